```python
import math
import jax, jax.numpy as jnp
from jax import lax
import numpy as np

D_MODEL = 1024
BATCH = 2
SEQ = 8192
DEPTH = 1
DEC_BATCH = 128
DEC_SEQ = 8
PAST_LEN = 8192
PAGE_SIZE = 128

D_MIX = D_MODEL
D_ATTN = D_MIX // 2
D_CONV = D_MIX - D_ATTN
ATTN_HALF_DIM = 64
ATTN_VDIM = 2 * ATTN_HALF_DIM
N_ATTN_HEADS = D_ATTN // ATTN_VDIM
CONV_WIDTH = 3
N_BUCKETS = 32
MAX_DISTANCE = 128
Q_BLOCK = 128
NORM_EPS = 1e-6
SUBLN_EPS = 1e-5
ATTN_SCALE = ATTN_HALF_DIM ** -0.5
NEG_INF = -1e30
D_IN = 4 * D_ATTN + 4 * D_CONV
SPLITS = (D_ATTN, 2 * D_ATTN, 3 * D_ATTN, 4 * D_ATTN,
          4 * D_ATTN + D_CONV, 4 * D_ATTN + 2 * D_CONV, 4 * D_ATTN + 3 * D_CONV)

kernel_name = "hybrid_shortconv_diffattn_adaln_step"


def rms_norm(x, eps=NORM_EPS):
    xf = x.astype(jnp.float32)
    y = xf * lax.rsqrt(jnp.mean(xf * xf, axis=-1, keepdims=True) + eps)
    return y.astype(x.dtype)


def t5_bucket(q_pos, k_pos):
    n = jnp.maximum(q_pos[:, None] - k_pos[None, :], 0)
    max_exact = N_BUCKETS // 2
    n_f = jnp.maximum(n, 1).astype(jnp.float32)
    large = max_exact + (jnp.log(n_f / max_exact) / math.log(MAX_DISTANCE / max_exact)
                         * (N_BUCKETS - max_exact)).astype(jnp.int32)
    large = jnp.minimum(large, N_BUCKETS - 1)
    return jnp.where(n < max_exact, n, large)


def diff_lambda(lq1, lk1, lq2, lk2, lam_init):
    f32 = jnp.float32
    return (jnp.exp(jnp.sum(lq1.astype(f32) * lk1.astype(f32)))
            - jnp.exp(jnp.sum(lq2.astype(f32) * lk2.astype(f32))) + lam_init)


def diff_weights(s, q_pos, k_pos, lam, rel_bias):
    bias = rel_bias.astype(jnp.float32)[t5_bucket(q_pos, k_pos)]
    s = s + jnp.transpose(bias, (2, 0, 1))[None, :, None]
    s = jnp.where(k_pos[None, :] <= q_pos[:, None], s, NEG_INF)
    p = jax.nn.softmax(s, axis=-1)
    return p[:, :, 0] - lam * p[:, :, 1]


def pre_mix(x, c, w_ada, b_ada, w_in):
    mod = jax.nn.silu(c) @ w_ada + b_ada
    shift, scale, gate = jnp.split(mod, 3, axis=-1)
    h = rms_norm(x) * (1 + scale[:, None, :]) + shift[:, None, :]
    parts = jnp.split(h @ w_in, SPLITS, axis=-1)
    return gate, parts


def short_conv(u_ext, w_conv, t):
    out = w_conv[0] * u_ext[:, 0:t]
    for j in range(1, CONV_WIDTH):
        out = out + w_conv[j] * u_ext[:, j:j + t]
    return out


def post_mix(x, gate, o_attn, z_attn, b_gate, conv, z_conv, subln_w, lam_init, w_out):
    b, t = x.shape[:2]
    o = rms_norm(o_attn, SUBLN_EPS) * subln_w * (1 - lam_init)
    a = o.reshape(b, t, D_ATTN) * jax.nn.silu(z_attn)
    g = b_gate * conv * jax.nn.silu(z_conv)
    out = jnp.concatenate([a, g], axis=-1) @ w_out
    return x + gate[:, None, :] * out


def prompt_layer(x, c, w_ada, b_ada, w_in, w_conv, lam, lam_init, subln_w, w_out, rel_bias):
    b, t = x.shape[:2]
    gate, (q, k, v, z_attn, b_gate, c_gate, h_conv, z_conv) = pre_mix(x, c, w_ada, b_ada, w_in)
    q = q.reshape(b, t, N_ATTN_HEADS, 2, ATTN_HALF_DIM)
    k = k.reshape(b, t, N_ATTN_HEADS, 2, ATTN_HALF_DIM)
    v = v.reshape(b, t, N_ATTN_HEADS, ATTN_VDIM)
    pos = jnp.arange(t)
    n_blocks = t // Q_BLOCK
    q_blocks = jnp.moveaxis(q.reshape(b, n_blocks, Q_BLOCK, N_ATTN_HEADS, 2, ATTN_HALF_DIM), 1, 0)

    def attend(args):
        q_i, i = args
        q_pos = i * Q_BLOCK + jnp.arange(Q_BLOCK)
        s = jnp.einsum('bqhmd,bkhmd->bhmqk', q_i, k,
                       preferred_element_type=jnp.float32) * ATTN_SCALE
        w = diff_weights(s, q_pos, pos, lam, rel_bias)
        return jnp.einsum('bhqk,bkhe->bqhe', w.astype(v.dtype), v)

    o = lax.map(attend, (q_blocks, jnp.arange(n_blocks)))
    o = jnp.moveaxis(o, 0, 1).reshape(b, t, N_ATTN_HEADS, ATTN_VDIM)
    u = c_gate * h_conv
    u_ext = jnp.concatenate([jnp.zeros((b, CONV_WIDTH - 1, D_CONV), u.dtype), u], axis=1)
    conv = short_conv(u_ext, w_conv, t)
    x = post_mix(x, gate, o, z_attn, b_gate, conv, z_conv, subln_w, lam_init, w_out)
    new_k = k.reshape(b, t, N_ATTN_HEADS, ATTN_VDIM)
    return x, new_k, v, u_ext[:, -(CONV_WIDTH - 1):]


def sample_layer(x, c, cache_k, cache_v, conv_state, page_table, w_ada, b_ada, w_in, w_conv,
                 lam, lam_init, subln_w, w_out, rel_bias):
    b, t = x.shape[:2]
    past_len = page_table.shape[1] * cache_k.shape[1]
    gate, (q, k, v, z_attn, b_gate, c_gate, h_conv, z_conv) = pre_mix(x, c, w_ada, b_ada, w_in)
    q = q.reshape(b, t, N_ATTN_HEADS, 2, ATTN_HALF_DIM)
    k = k.reshape(b, t, N_ATTN_HEADS, 2, ATTN_HALF_DIM)
    v = v.reshape(b, t, N_ATTN_HEADS, ATTN_VDIM)
    k_past = cache_k[page_table].reshape(b, past_len, N_ATTN_HEADS, 2, ATTN_HALF_DIM)
    v_past = cache_v[page_table].reshape(b, past_len, N_ATTN_HEADS, ATTN_VDIM)
    s = jnp.concatenate([
        jnp.einsum('bqhmd,bkhmd->bhmqk', q, k_past.astype(q.dtype), preferred_element_type=jnp.float32),
        jnp.einsum('bqhmd,bkhmd->bhmqk', q, k, preferred_element_type=jnp.float32)], axis=-1) * ATTN_SCALE
    q_pos = past_len + jnp.arange(t)
    k_pos = jnp.arange(past_len + t)
    w = diff_weights(s, q_pos, k_pos, lam, rel_bias).astype(v.dtype)
    o = (jnp.einsum('bhqk,bkhe->bqhe', w[..., :past_len], v_past.astype(v.dtype))
         + jnp.einsum('bhqk,bkhe->bqhe', w[..., past_len:], v))
    u = c_gate * h_conv
    u_ext = jnp.concatenate([conv_state.astype(u.dtype), u], axis=1)
    conv = short_conv(u_ext, w_conv, t)
    x = post_mix(x, gate, o, z_attn, b_gate, conv, z_conv, subln_w, lam_init, w_out)
    new_k = k.reshape(b, t, N_ATTN_HEADS, ATTN_VDIM)
    return x, new_k, v, u_ext[:, -(CONV_WIDTH - 1):]


def setup_inputs(seed: int = 0) -> dict:
    key = jax.random.key(seed)
    ks = jax.random.split(key, 24)
    n_pages = PAST_LEN // PAGE_SIZE
    n_used = DEC_BATCH * n_pages
    n_phys = n_used + (n_used + 3) // 4
    page_table = jax.random.permutation(ks[5], n_phys)[:n_used].reshape(DEC_BATCH, n_pages).astype(jnp.int32)
    nrm = jax.random.normal
    f32 = jnp.float32
    return {
        "x_prompt": nrm(ks[0], (BATCH, SEQ, D_MODEL), f32),
        "x_sample": nrm(ks[1], (DEC_BATCH, DEC_SEQ, D_MODEL), f32),
        "cache_k": nrm(ks[2], (DEPTH, n_phys, PAGE_SIZE, N_ATTN_HEADS, ATTN_VDIM), f32),
        "cache_v": nrm(ks[3], (DEPTH, n_phys, PAGE_SIZE, N_ATTN_HEADS, ATTN_VDIM), f32),
        "state_conv": nrm(ks[4], (DEPTH, DEC_BATCH, CONV_WIDTH - 1, D_CONV), f32),
        "page_table": page_table,
        "c_prompt": nrm(ks[6], (BATCH, D_MODEL), f32),
        "c_sample": nrm(ks[7], (DEC_BATCH, D_MODEL), f32),
        "rel_bias": 0.5 * nrm(ks[8], (N_BUCKETS, N_ATTN_HEADS), f32),
        "w_ada": nrm(ks[9], (DEPTH, D_MODEL, 3 * D_MODEL), f32) * D_MODEL ** -0.5,
        "b_ada": 0.02 * nrm(ks[10], (DEPTH, 3 * D_MODEL), f32),
        "w_in": nrm(ks[11], (DEPTH, D_MODEL, D_IN), f32) * D_MODEL ** -0.5,
        "w_conv": nrm(ks[12], (DEPTH, CONV_WIDTH, D_CONV), f32) * CONV_WIDTH ** -0.5,
        "lambda_q1": 0.1 * nrm(ks[13], (DEPTH, ATTN_HALF_DIM), f32),
        "lambda_k1": 0.1 * nrm(ks[14], (DEPTH, ATTN_HALF_DIM), f32),
        "lambda_q2": 0.1 * nrm(ks[15], (DEPTH, ATTN_HALF_DIM), f32),
        "lambda_k2": 0.1 * nrm(ks[16], (DEPTH, ATTN_HALF_DIM), f32),
        "subln_w": 1.0 + 0.02 * nrm(ks[17], (DEPTH, ATTN_VDIM), f32),
        "w_out": nrm(ks[18], (DEPTH, D_MIX, D_MODEL), f32) * D_MIX ** -0.5,
        "norm_f": 1.0 + 0.02 * nrm(ks[19], (D_MODEL,), f32),
    }


def reference(x_prompt, x_sample, cache_k, cache_v, state_conv, page_table, c_prompt, c_sample,
              rel_bias, w_ada, b_ada, w_in, w_conv, lambda_q1, lambda_k1, lambda_q2, lambda_k2,
              subln_w, w_out, norm_f):
    xp, xs = x_prompt, x_sample
    kp_l, vp_l, sp_l, ks_l, vs_l, ss_l = [], [], [], [], [], []
    for l in range(DEPTH):
        lam_init = 0.8 - 0.6 * math.exp(-0.3 * l)
        lam = diff_lambda(lambda_q1[l], lambda_k1[l], lambda_q2[l], lambda_k2[l], lam_init)
        xp, kp, vp, sp = prompt_layer(xp, c_prompt, w_ada[l], b_ada[l], w_in[l], w_conv[l],
                                      lam, lam_init, subln_w[l], w_out[l], rel_bias)
        xs, k_s, v_s, s_s = sample_layer(xs, c_sample, cache_k[l], cache_v[l], state_conv[l],
                                         page_table, w_ada[l], b_ada[l], w_in[l], w_conv[l],
                                         lam, lam_init, subln_w[l], w_out[l], rel_bias)
        kp_l.append(kp); vp_l.append(vp); sp_l.append(sp)
        ks_l.append(k_s); vs_l.append(v_s); ss_l.append(s_s)
    y_prompt = rms_norm(xp) * norm_f
    y_sample = rms_norm(xs) * norm_f
    return (y_prompt, y_sample, jnp.stack(kp_l), jnp.stack(vp_l), jnp.stack(sp_l),
            jnp.stack(ks_l), jnp.stack(vs_l), jnp.stack(ss_l))
```

```python
import functools
import math

import jax
import jax.numpy as jnp
from jax import lax
from jax.experimental import pallas as pl
from jax.experimental.pallas import tpu as pltpu

F32 = jnp.float32
BF16 = jnp.bfloat16

D_MODEL = 1024
D_ATTN = 512
D_CONV = 512
HALF_DIM = 64
VDIM = 128
N_HEADS = 4
CONV_WIDTH = 3
N_BUCKETS = 32
MAX_DISTANCE = 128
NORM_EPS = 1e-6
SUBLN_EPS = 1e-5
ATTN_SCALE = HALF_DIM ** -0.5
NEG_INF = -1e30
D_IN = 4 * D_ATTN + 4 * D_CONV

LANES = 128
VMEM_LIMIT = 56 * 1024 * 1024

MOD_ROWS_PAD = 8
MOD_TN = 512
ROW_TILE = 512
TQ = 256
TK = 256
PAGES_PER_STEP = 8


def _silu(x):
    return x * jax.nn.sigmoid(x)


def _dot_nt(a, b):
    return lax.dot_general(a, b, (((1,), (1,)), ((), ())), preferred_element_type=F32)


def _lane_tile(x, n):
    return x if n == 1 else jnp.concatenate([x] * n, axis=1)


def _mod_body(c_ref, w_ref, b_ref, o_ref):
    c = c_ref[...]
    o_ref[...] = jnp.dot(_silu(c).astype(BF16), w_ref[...].astype(BF16),
                         preferred_element_type=F32) + b_ref[...]


def _modulation(c_all, w_ada, b_ada):
    rows = c_all.shape[0]
    n = w_ada.shape[1]
    return pl.pallas_call(
        _mod_body,
        out_shape=jax.ShapeDtypeStruct((rows, n), F32),
        grid=(n // MOD_TN,),
        in_specs=[pl.BlockSpec((rows, D_MODEL), lambda j: (0, 0)),
                  pl.BlockSpec((D_MODEL, MOD_TN), lambda j: (0, j)),
                  pl.BlockSpec((1, MOD_TN), lambda j: (0, j))],
        out_specs=pl.BlockSpec((rows, MOD_TN), lambda j: (0, j)),
        compiler_params=pltpu.CompilerParams(dimension_semantics=("arbitrary",),
                                             vmem_limit_bytes=VMEM_LIMIT),
        name="adaln_mod",
    )(c_all, w_ada, b_ada)


def _premix_body(x_ref, shift_ref, scale_ref, w_ref, wconv_ref, prev_ref, *rest, g, r, attn_dtype,
                 emit_bf16_kv):
    if emit_bf16_kv:
        qa_ref, qb_ref, kf_ref, vf_ref, kb_ref, vb_ref, sz_ref, g_ref, st_ref, carry = rest
    else:
        qa_ref, qb_ref, kf_ref, vf_ref, sz_ref, g_ref, st_ref, carry = rest
    rows = g * r

    @pl.when(pl.program_id(1) == 0)
    def _():
        carry[...] = prev_ref[...]

    x = x_ref[...]
    xn = x * lax.rsqrt(jnp.mean(x * x, axis=-1, keepdims=True) + NORM_EPS)
    h = xn * (1.0 + scale_ref[...]) + shift_ref[...]
    h2 = h.reshape(rows, D_MODEL).astype(BF16)

    def proj(i):
        return jnp.dot(h2, w_ref[:, i * D_ATTN:(i + 1) * D_ATTN], preferred_element_type=F32)

    def put(ref, val):
        ref[...] = val.reshape(g, r, val.shape[-1]).astype(ref.dtype)

    q = proj(0) * ATTN_SCALE
    lane = lax.broadcasted_iota(jnp.int32, (rows, D_ATTN), 1)
    first_half = (lane & (VDIM - 1)) < HALF_DIM
    put(qa_ref, jnp.where(first_half, q, 0.0))
    put(qb_ref, jnp.where(first_half, 0.0, q))
    k = proj(1)
    put(kf_ref, k)
    v = proj(2)
    put(vf_ref, v)
    if emit_bf16_kv:
        put(kb_ref, k)
        put(vb_ref, v)
    put(sz_ref, _silu(proj(3)))

    b_gate = proj(4)
    u = proj(5) * proj(6)
    prev = carry[...]
    prev0 = jnp.broadcast_to(prev[:, 0:1, :], (g, r, D_CONV)).reshape(rows, D_CONV)
    prev1 = jnp.broadcast_to(prev[:, 1:2, :], (g, r, D_CONV)).reshape(rows, D_CONV)
    t = lax.broadcasted_iota(jnp.int32, (rows, D_CONV), 0) & (r - 1)
    u1 = jnp.where(t >= 1, pltpu.roll(u, 1, axis=0), prev1)
    u2 = jnp.where(t >= 2, pltpu.roll(u, 2, axis=0), jnp.where(t == 1, prev1, prev0))
    wc = wconv_ref[...]
    conv = wc[0:1, :] * u2 + wc[1:2, :] * u1 + wc[2:3, :] * u
    put(g_ref, b_gate * conv * _silu(proj(7)))

    last2 = u.reshape(g, r, D_CONV)[:, r - 2:, :]
    carry[...] = last2
    st_ref[...] = last2


def _premix(x3, mod3, w_in_bf, w_conv, prev_state, *, g, r, attn_dtype, emit_bf16_kv):
    n_groups, rows_per_group, _ = x3.shape
    grid = (n_groups // g, rows_per_group // r)
    assert r & (r - 1) == 0 and r >= 8
    act = lambda n: pl.BlockSpec((g, r, n), lambda o, i: (o, i, 0))
    mod_spec = lambda col: pl.BlockSpec((g, 1, D_MODEL), lambda o, i: (o, 0, col))
    sds = lambda dt: jax.ShapeDtypeStruct((n_groups, rows_per_group, D_ATTN), dt)
    out_shape = [sds(attn_dtype), sds(attn_dtype), sds(F32), sds(F32)]
    if emit_bf16_kv:
        out_shape += [sds(BF16), sds(BF16)]
    out_shape += [sds(attn_dtype), sds(attn_dtype),
                  jax.ShapeDtypeStruct((n_groups, CONV_WIDTH - 1, D_CONV), F32)]
    out_specs = [act(D_ATTN)] * (len(out_shape) - 1)
    out_specs.append(pl.BlockSpec((g, CONV_WIDTH - 1, D_CONV), lambda o, i: (o, 0, 0)))
    body = functools.partial(_premix_body, g=g, r=r, attn_dtype=attn_dtype,
                             emit_bf16_kv=emit_bf16_kv)
    return pl.pallas_call(
        body,
        out_shape=out_shape,
        grid=grid,
        in_specs=[act(D_MODEL), mod_spec(0), mod_spec(1),
                  pl.BlockSpec((D_MODEL, D_IN), lambda o, i: (0, 0)),
                  pl.BlockSpec((CONV_WIDTH, D_CONV), lambda o, i: (0, 0)),
                  pl.BlockSpec((g, CONV_WIDTH - 1, D_CONV), lambda o, i: (o, 0, 0))],
        out_specs=out_specs,
        scratch_shapes=[pltpu.VMEM((g, CONV_WIDTH - 1, D_CONV), F32)],
        compiler_params=pltpu.CompilerParams(dimension_semantics=("arbitrary", "arbitrary"),
                                             vmem_limit_bytes=VMEM_LIMIT),
        name="premix",
    )(x3, mod3, mod3, w_in_bf, w_conv, prev_state)


def _postmix_body(a_ref, g_ref, x_ref, gate_ref, w_ref, nf_ref, y_ref, *, g, r):
    rows = g * r
    a = a_ref[...].reshape(rows, D_ATTN).astype(BF16)
    gg = g_ref[...].reshape(rows, D_CONV).astype(BF16)
    mix = jnp.concatenate([a, gg], axis=1)
    out = jnp.dot(mix, w_ref[...], preferred_element_type=F32).reshape(g, r, D_MODEL)
    xo = x_ref[...] + gate_ref[...] * out
    y = xo * lax.rsqrt(jnp.mean(xo * xo, axis=-1, keepdims=True) + NORM_EPS)
    y_ref[...] = y * nf_ref[...]


def _postmix(a3, g3, x3, mod3, w_out_bf, norm_f, *, g, r):
    n_groups, rows_per_group, _ = x3.shape
    grid = (n_groups // g, rows_per_group // r)
    act = lambda n: pl.BlockSpec((g, r, n), lambda o, i: (o, i, 0))
    return pl.pallas_call(
        functools.partial(_postmix_body, g=g, r=r),
        out_shape=jax.ShapeDtypeStruct(x3.shape, F32),
        grid=grid,
        in_specs=[act(D_ATTN), act(D_CONV), act(D_MODEL),
                  pl.BlockSpec((g, 1, D_MODEL), lambda o, i: (o, 0, 2)),
                  pl.BlockSpec((D_ATTN + D_CONV, D_MODEL), lambda o, i: (0, 0)),
                  pl.BlockSpec((1, 1, D_MODEL), lambda o, i: (0, 0, 0))],
        out_specs=act(D_MODEL),
        compiler_params=pltpu.CompilerParams(dimension_semantics=("arbitrary", "arbitrary"),
                                             vmem_limit_bytes=VMEM_LIMIT),
        name="postmix",
    )(a3, g3, x3, mod3, w_out_bf, norm_f.reshape(1, 1, D_MODEL))


def _lambda(lq1_ref, lk1_ref, lq2_ref, lk2_ref, lam_init):
    l1 = jnp.sum(lq1_ref[...] * lk1_ref[...], axis=1, keepdims=True)
    l2 = jnp.sum(lq2_ref[...] * lk2_ref[...], axis=1, keepdims=True)
    return jnp.exp(l1) - jnp.exp(l2) + lam_init


def _sub_ln_gate(o, subln_ref, sz, lam_init):
    on = o * lax.rsqrt(jnp.mean(o * o, axis=-1, keepdims=True) + SUBLN_EPS)
    return on * (subln_ref[...] * (1.0 - lam_init)) * sz


def _pattn_body(qa_ref, qb_ref, k_ref, v_ref, bias_ref, sz_ref, lq1_ref, lk1_ref, lq2_ref,
                lk2_ref, subln_ref, o_ref, m_scr, acc_scr, *, lam_init):
    qi = pl.program_id(2)
    q2 = jnp.concatenate([qa_ref[0], qb_ref[0]], axis=0)
    m_scr[...] = jnp.full(m_scr.shape, NEG_INF, F32)
    acc_scr[...] = jnp.zeros(acc_scr.shape, F32)

    def tile(off, width, bias):
        k_t = k_ref[0, pl.ds(off, width), :]
        v_t = v_ref[0, pl.ds(off, width), :]
        s = _dot_nt(q2, k_t)
        if bias is not None:
            s = s + jnp.concatenate([bias, bias], axis=0)
        m_prev = m_scr[...]
        m_new = jnp.maximum(m_prev, jnp.max(s, axis=1, keepdims=True))
        alpha = jnp.exp(m_prev - m_new)
        p = jnp.exp(s - _lane_tile(m_new, width // LANES)).astype(BF16)
        v_aug = jnp.concatenate([v_t, jnp.ones((width, LANES), BF16)], axis=1)
        pv = jnp.dot(p, v_aug, preferred_element_type=F32)
        acc_scr[...] = acc_scr[...] * _lane_tile(alpha, 2) + pv
        m_scr[...] = m_new

    n_far = jnp.maximum(qi - 1, 0)

    def far_body(j, carry):
        tile(pl.multiple_of(j * TK, TK), TK, None)
        return carry

    lax.fori_loop(0, n_far, far_body, 0)
    tile(pl.multiple_of(n_far * TK, TK), 2 * TQ, bias_ref[0, 0])

    acc = acc_scr[...]
    o_all = acc[:, :VDIM] / acc[:, VDIM:]
    lam = _lambda(lq1_ref, lk1_ref, lq2_ref, lk2_ref, lam_init)
    o = o_all[:TQ] - lam * o_all[TQ:]
    o_ref[0] = _sub_ln_gate(o, subln_ref, sz_ref[0].astype(F32), lam_init).astype(o_ref.dtype)


def _prompt_attention(qa, qb, kb, vb, bias, sz, lams, subln, *, lam_init):
    b, t, _ = qa.shape
    qspec = pl.BlockSpec((1, TQ, VDIM), lambda bi, h, qi: (bi, qi, h))
    kvspec = pl.BlockSpec((1, t, VDIM), lambda bi, h, qi: (bi, 0, h))
    small = lambda n: pl.BlockSpec((1, n), lambda bi, h, qi: (0, 0))
    return pl.pallas_call(
        functools.partial(_pattn_body, lam_init=lam_init),
        out_shape=jax.ShapeDtypeStruct((b, t, D_ATTN), BF16),
        grid=(b, N_HEADS, t // TQ),
        in_specs=[qspec, qspec, kvspec, kvspec,
                  pl.BlockSpec((1, 1, TQ, 2 * TQ), lambda bi, h, qi: (jnp.minimum(qi, 1), h, 0, 0)),
                  qspec, small(HALF_DIM), small(HALF_DIM), small(HALF_DIM), small(HALF_DIM),
                  small(VDIM)],
        out_specs=qspec,
        scratch_shapes=[pltpu.VMEM((2 * TQ, LANES), F32), pltpu.VMEM((2 * TQ, 2 * LANES), F32)],
        compiler_params=pltpu.CompilerParams(
            dimension_semantics=("arbitrary", "arbitrary", "arbitrary"),
            vmem_limit_bytes=VMEM_LIMIT),
        name="prompt_attention",
    )(qa, qb, kb, vb, bias, sz, *lams, subln)


def _sattn_body(pt_ref, qa_ref, qb_ref, kn_ref, vn_ref, sz_ref, biasl_ref, biasn_ref, lq1_ref,
                lk1_ref, lq2_ref, lk2_ref, subln_ref, *rest, n_pages, page, n_chunks, lam_init):
    k_refs = rest[:n_pages]
    v_refs = rest[n_pages:2 * n_pages]
    o_ref, m_scr, l_scr, acc_scr = rest[2 * n_pages:]
    c = pl.program_id(1)
    rows = qa_ref.shape[1]

    @pl.when(c == 0)
    def _():
        m_scr[...] = jnp.full(m_scr.shape, NEG_INF, F32)
        l_scr[...] = jnp.zeros(l_scr.shape, F32)
        acc_scr[...] = jnp.zeros(acc_scr.shape, F32)

    def head_slice(h):
        return slice(h * VDIM, (h + 1) * VDIM)

    def q_stack(h):
        return jnp.concatenate([qa_ref[0][:, head_slice(h)], qb_ref[0][:, head_slice(h)]],
                               axis=0).astype(BF16)

    def update(h, s, v_parts):
        m_prev = m_scr[h]
        m_new = jnp.maximum(m_prev, jnp.max(s, axis=1, keepdims=True))
        alpha = jnp.exp(m_prev - m_new)
        p = jnp.exp(s - _lane_tile(m_new, len(v_parts)))
        l_scr[h] = alpha * l_scr[h] + jnp.sum(p, axis=1, keepdims=True)
        pb = p.astype(BF16)
        pv = None
        for i, v_t in enumerate(v_parts):
            d = jnp.dot(pb[:, i * page:(i + 1) * page], v_t, preferred_element_type=F32)
            pv = d if pv is None else pv + d
        acc_scr[h] = alpha * acc_scr[h] + pv
        m_scr[h] = m_new

    for h in range(N_HEADS):
        q2 = q_stack(h)
        s = jnp.concatenate([_dot_nt(q2, k_refs[i][:, head_slice(h)].astype(BF16))
                             for i in range(n_pages)], axis=1)
        s = s + biasl_ref[0, h]
        update(h, s, [v_refs[i][:, head_slice(h)].astype(BF16) for i in range(n_pages)])

    @pl.when(c == n_chunks - 1)
    def _():
        lam = _lambda(lq1_ref, lk1_ref, lq2_ref, lk2_ref, lam_init)
        pad = jnp.zeros((page - rows, VDIM), F32)
        for h in range(N_HEADS):
            k_new = jnp.concatenate([kn_ref[0][:, head_slice(h)], pad], axis=0).astype(BF16)
            v_new = jnp.concatenate([vn_ref[0][:, head_slice(h)], pad], axis=0).astype(BF16)
            update(h, _dot_nt(q_stack(h), k_new) + biasn_ref[h], [v_new])
            o_all = acc_scr[h] / l_scr[h]
            o = o_all[:rows] - lam * o_all[rows:]
            o_ref[0, :, head_slice(h)] = _sub_ln_gate(o, subln_ref, sz_ref[0][:, head_slice(h)],
                                                      lam_init)


def _sample_attention(page_table, qa, qb, kn, vn, sz, biasl, biasn, lams, subln, cache_k, cache_v,
                      *, lam_init):
    nb, rows, _ = qa.shape
    n_phys, page, width = cache_k.shape
    n_pages_total = page_table.shape[1]
    n_chunks = n_pages_total // PAGES_PER_STEP
    act = pl.BlockSpec((1, rows, D_ATTN), lambda b, c, pt: (b, 0, 0))
    small = lambda n: pl.BlockSpec((1, n), lambda b, c, pt: (0, 0))

    def page_spec(i):
        return pl.BlockSpec((None, page, width),
                            lambda b, c, pt: (pt[b, c * PAGES_PER_STEP + i], 0, 0))

    pages = [page_spec(i) for i in range(PAGES_PER_STEP)]
    grid_spec = pltpu.PrefetchScalarGridSpec(
        num_scalar_prefetch=1,
        grid=(nb, n_chunks),
        in_specs=[act, act, act, act, act,
                  pl.BlockSpec((1, N_HEADS, 2 * rows, PAGES_PER_STEP * page),
                               lambda b, c, pt: (c // (n_chunks - 1), 0, 0, 0)),
                  pl.BlockSpec((N_HEADS, 2 * rows, page), lambda b, c, pt: (0, 0, 0)),
                  small(HALF_DIM), small(HALF_DIM), small(HALF_DIM), small(HALF_DIM), small(VDIM)]
                 + pages + pages,
        out_specs=act,
        scratch_shapes=[pltpu.VMEM((N_HEADS, 2 * rows, LANES), F32)] * 3,
    )
    body = functools.partial(_sattn_body, n_pages=PAGES_PER_STEP, page=page, n_chunks=n_chunks,
                             lam_init=lam_init)
    return pl.pallas_call(
        body,
        out_shape=jax.ShapeDtypeStruct((nb, rows, D_ATTN), F32),
        grid_spec=grid_spec,
        compiler_params=pltpu.CompilerParams(dimension_semantics=("arbitrary", "arbitrary"),
                                             vmem_limit_bytes=VMEM_LIMIT),
        name="sample_attention",
    )(page_table, qa, qb, kn, vn, sz, biasl, biasn, *lams, subln,
      *([cache_k] * PAGES_PER_STEP), *([cache_v] * PAGES_PER_STEP))


def _bias_by_distance(rel_bias, n_max):
    n = jnp.arange(n_max)
    max_exact = N_BUCKETS // 2
    n_f = jnp.maximum(n, 1).astype(F32)
    large = max_exact + (jnp.log(n_f / max_exact) / math.log(MAX_DISTANCE / max_exact)
                         * (N_BUCKETS - max_exact)).astype(jnp.int32)
    bucket = jnp.where(n < max_exact, n, jnp.minimum(large, N_BUCKETS - 1))
    table = rel_bias.astype(F32)
    return (table[bucket] - table[N_BUCKETS - 1]).T


def _toeplitz(bias_d, q_pos, k_pos):
    d = q_pos[:, None] - k_pos[None, :]
    tile = bias_d[:, jnp.clip(d, 0, bias_d.shape[1] - 1)]
    return jnp.where(d >= 0, tile, NEG_INF)


def kernel(x_prompt, x_sample, cache_k, cache_v, state_conv, page_table, c_prompt, c_sample,
           rel_bias, w_ada, b_ada, w_in, w_conv, lambda_q1, lambda_k1, lambda_q2, lambda_k2,
           subln_w, w_out, norm_f):
    depth = w_in.shape[0]
    assert depth == 1
    batch, seq, _ = x_prompt.shape
    nb, dec_seq, _ = x_sample.shape
    n_phys, page = cache_k.shape[1], cache_k.shape[2]
    past_len = page_table.shape[1] * page
    lam_init = 0.8 - 0.6 * math.exp(-0.3 * 0)

    n_c = batch + nb
    n_c_pad = -(-n_c // MOD_ROWS_PAD) * MOD_ROWS_PAD
    c_all = jnp.concatenate([c_prompt, c_sample, jnp.zeros((n_c_pad - n_c, D_MODEL), F32)], axis=0)
    mod = _modulation(c_all, w_ada[0], b_ada[0].reshape(1, -1))
    mod_p = mod[:batch].reshape(batch, 1, 3 * D_MODEL)
    mod_s = mod[batch:n_c].reshape(nb, 1, 3 * D_MODEL)

    w_in_bf = w_in[0].astype(BF16)
    w_out_bf = w_out[0].astype(BF16)
    lams = [a.reshape(1, HALF_DIM) for a in (lambda_q1[0], lambda_k1[0], lambda_q2[0], lambda_k2[0])]
    subln = subln_w[0].reshape(1, VDIM)

    bias_d = _bias_by_distance(rel_bias, 2 * TQ + dec_seq + page)
    r_q = jnp.arange(TQ)
    near = _toeplitz(bias_d, TQ + r_q, jnp.arange(2 * TQ))
    first = jnp.concatenate([_toeplitz(bias_d, r_q, jnp.arange(TQ)),
                             jnp.full((N_HEADS, TQ, TQ), NEG_INF, F32)], axis=2)
    bias_prompt = jnp.stack([first, near])
    q_pos = past_len + jnp.arange(dec_seq)
    last_page = _toeplitz(bias_d, q_pos, past_len - page + jnp.arange(page))
    chunk_w = PAGES_PER_STEP * page
    last_chunk = jnp.concatenate([jnp.zeros((N_HEADS, dec_seq, chunk_w - page), F32), last_page],
                                 axis=2)
    last_chunk = jnp.concatenate([last_chunk, last_chunk], axis=1)
    bias_last = jnp.stack([jnp.zeros_like(last_chunk), last_chunk])
    new_pos = jnp.where(jnp.arange(page) < dec_seq, past_len + jnp.arange(page), past_len + 2 * page)
    bias_new = _toeplitz(bias_d, q_pos, new_pos)
    bias_new = jnp.concatenate([bias_new, bias_new], axis=1)

    qa, qb, kf, vf, kb, vb, sz, gp, st_p = _premix(
        x_prompt, mod_p, w_in_bf, w_conv[0], jnp.zeros((batch, CONV_WIDTH - 1, D_CONV), F32),
        g=1, r=ROW_TILE, attn_dtype=BF16, emit_bf16_kv=True)
    a_p = _prompt_attention(qa, qb, kb, vb, bias_prompt, sz, lams, subln, lam_init=lam_init)
    y_prompt = _postmix(a_p, gp, x_prompt, mod_p, w_out_bf, norm_f, g=1, r=ROW_TILE)

    g_s = ROW_TILE // dec_seq
    qa_s, qb_s, kf_s, vf_s, sz_s, gs, st_s = _premix(
        x_sample, mod_s, w_in_bf, w_conv[0], state_conv[0],
        g=g_s, r=dec_seq, attn_dtype=F32, emit_bf16_kv=False)
    a_s = _sample_attention(page_table, qa_s, qb_s, kf_s, vf_s, sz_s, bias_last, bias_new, lams,
                            subln, cache_k[0].reshape(n_phys, page, N_HEADS * VDIM),
                            cache_v[0].reshape(n_phys, page, N_HEADS * VDIM), lam_init=lam_init)
    y_sample = _postmix(a_s, gs, x_sample, mod_s, w_out_bf, norm_f, g=g_s, r=dec_seq)

    kv_p = lambda a: a.reshape(1, batch, seq, N_HEADS, VDIM)
    kv_s = lambda a: a.reshape(1, nb, dec_seq, N_HEADS, VDIM)
    return (y_prompt, y_sample, kv_p(kf), kv_p(vf), st_p[None], kv_s(kf_s), kv_s(vf_s), st_s[None])
```

```python
import functools
import math

import jax
import jax.numpy as jnp
from jax import lax
from jax.experimental import pallas as pl
from jax.experimental.pallas import tpu as pltpu

F32 = jnp.float32
BF16 = jnp.bfloat16

D_MODEL = 1024
D_ATTN = 512
D_CONV = 512
HALF_DIM = 64
VDIM = 128
N_HEADS = 4
CONV_WIDTH = 3
N_BUCKETS = 32
MAX_DISTANCE = 128
NORM_EPS = 1e-6
SUBLN_EPS = 1e-5
ATTN_SCALE = HALF_DIM ** -0.5
LOG2_E = math.log2(math.e)
NEG_INF = -1e30
D_IN = 4 * D_ATTN + 4 * D_CONV

LANES = 128
VMEM_LIMIT = 56 * 1024 * 1024

MOD_ROWS_PAD = 8
MOD_TN = 512
ROW_TILE = 512
TQ = 512
TK = 256
NEAR_TILES = TQ // TK + 1
FAR_UNROLL = 4
PAGES_PER_STEP = 8


def _silu(x):
    return x * jax.nn.sigmoid(x)


def _dot_nt(a, b):
    return lax.dot_general(a, b, (((1,), (1,)), ((), ())), preferred_element_type=F32)


def _lane_tile(x, n):
    return x if n == 1 else jnp.concatenate([x] * n, axis=1)


def _row_tile(x):
    return jnp.concatenate([x, x], axis=0)


def _mod_body(c_ref, w_ref, b_ref, o_ref):
    c = c_ref[...]
    o_ref[...] = jnp.dot(_silu(c).astype(BF16), w_ref[...].astype(BF16),
                         preferred_element_type=F32) + b_ref[...]


def _modulation(c_all, w_ada, b_ada):
    rows = c_all.shape[0]
    n = w_ada.shape[1]
    return pl.pallas_call(
        _mod_body,
        out_shape=jax.ShapeDtypeStruct((rows, n), F32),
        grid=(n // MOD_TN,),
        in_specs=[pl.BlockSpec((rows, D_MODEL), lambda j: (0, 0)),
                  pl.BlockSpec((D_MODEL, MOD_TN), lambda j: (0, j)),
                  pl.BlockSpec((1, MOD_TN), lambda j: (0, j))],
        out_specs=pl.BlockSpec((rows, MOD_TN), lambda j: (0, j)),
        compiler_params=pltpu.CompilerParams(dimension_semantics=("arbitrary",),
                                             vmem_limit_bytes=VMEM_LIMIT),
        name="adaln_mod",
    )(c_all, w_ada, b_ada)


def _premix_body(x_ref, shift_ref, scale_ref, w_ref, wconv_ref, prev_ref, *rest, g, r, attn_dtype,
                 emit_bf16_kv):
    if emit_bf16_kv:
        qa_ref, qb_ref, kf_ref, vf_ref, kb_ref, vb_ref, sz_ref, g_ref, st_ref, carry = rest
    else:
        qa_ref, qb_ref, kf_ref, vf_ref, sz_ref, g_ref, st_ref, carry = rest
    rows = g * r

    @pl.when(pl.program_id(1) == 0)
    def _():
        carry[...] = prev_ref[...]

    x = x_ref[...]
    xn = x * lax.rsqrt(jnp.mean(x * x, axis=-1, keepdims=True) + NORM_EPS)
    h = xn * (1.0 + scale_ref[...]) + shift_ref[...]
    h2 = h.reshape(rows, D_MODEL).astype(BF16)

    def proj(i):
        return jnp.dot(h2, w_ref[:, i * D_ATTN:(i + 1) * D_ATTN], preferred_element_type=F32)

    def put(ref, val):
        ref[...] = val.reshape(g, r, val.shape[-1]).astype(ref.dtype)

    q = proj(0) * (ATTN_SCALE * LOG2_E)
    lane = lax.broadcasted_iota(jnp.int32, (rows, D_ATTN), 1)
    first_half = (lane & (VDIM - 1)) < HALF_DIM
    put(qa_ref, jnp.where(first_half, q, 0.0))
    put(qb_ref, jnp.where(first_half, 0.0, q))
    k = proj(1)
    put(kf_ref, k)
    v = proj(2)
    put(vf_ref, v)
    if emit_bf16_kv:
        put(kb_ref, k)
        put(vb_ref, v)
    put(sz_ref, _silu(proj(3)))

    b_gate = proj(4)
    u = proj(5) * proj(6)
    prev = carry[...]
    prev0 = jnp.broadcast_to(prev[:, 0:1, :], (g, r, D_CONV)).reshape(rows, D_CONV)
    prev1 = jnp.broadcast_to(prev[:, 1:2, :], (g, r, D_CONV)).reshape(rows, D_CONV)
    t = lax.broadcasted_iota(jnp.int32, (rows, D_CONV), 0) & (r - 1)
    u1 = jnp.where(t >= 1, pltpu.roll(u, 1, axis=0), prev1)
    u2 = jnp.where(t >= 2, pltpu.roll(u, 2, axis=0), jnp.where(t == 1, prev1, prev0))
    wc = wconv_ref[...]
    conv = wc[0:1, :] * u2 + wc[1:2, :] * u1 + wc[2:3, :] * u
    put(g_ref, b_gate * conv * _silu(proj(7)))

    last2 = u.reshape(g, r, D_CONV)[:, r - 2:, :]
    carry[...] = last2
    st_ref[...] = last2


def _premix(x3, mod3, w_in_bf, w_conv, prev_state, *, g, r, attn_dtype, emit_bf16_kv):
    n_groups, rows_per_group, _ = x3.shape
    grid = (n_groups // g, rows_per_group // r)
    assert r & (r - 1) == 0 and r >= 8
    act = lambda n: pl.BlockSpec((g, r, n), lambda o, i: (o, i, 0))
    mod_spec = lambda col: pl.BlockSpec((g, 1, D_MODEL), lambda o, i: (o, 0, col))
    sds = lambda dt: jax.ShapeDtypeStruct((n_groups, rows_per_group, D_ATTN), dt)
    out_shape = [sds(attn_dtype), sds(attn_dtype), sds(F32), sds(F32)]
    if emit_bf16_kv:
        out_shape += [sds(BF16), sds(BF16)]
    out_shape += [sds(attn_dtype), sds(attn_dtype),
                  jax.ShapeDtypeStruct((n_groups, CONV_WIDTH - 1, D_CONV), F32)]
    out_specs = [act(D_ATTN)] * (len(out_shape) - 1)
    out_specs.append(pl.BlockSpec((g, CONV_WIDTH - 1, D_CONV), lambda o, i: (o, 0, 0)))
    body = functools.partial(_premix_body, g=g, r=r, attn_dtype=attn_dtype,
                             emit_bf16_kv=emit_bf16_kv)
    return pl.pallas_call(
        body,
        out_shape=out_shape,
        grid=grid,
        in_specs=[act(D_MODEL), mod_spec(0), mod_spec(1),
                  pl.BlockSpec((D_MODEL, D_IN), lambda o, i: (0, 0)),
                  pl.BlockSpec((CONV_WIDTH, D_CONV), lambda o, i: (0, 0)),
                  pl.BlockSpec((g, CONV_WIDTH - 1, D_CONV), lambda o, i: (o, 0, 0))],
        out_specs=out_specs,
        scratch_shapes=[pltpu.VMEM((g, CONV_WIDTH - 1, D_CONV), F32)],
        compiler_params=pltpu.CompilerParams(dimension_semantics=("arbitrary", "arbitrary"),
                                             vmem_limit_bytes=VMEM_LIMIT),
        name="premix",
    )(x3, mod3, mod3, w_in_bf, w_conv, prev_state)


def _postmix_body(a_ref, g_ref, x_ref, gate_ref, w_ref, nf_ref, y_ref, *, g, r):
    rows = g * r
    a = a_ref[...].reshape(rows, D_ATTN).astype(BF16)
    gg = g_ref[...].reshape(rows, D_CONV).astype(BF16)
    mix = jnp.concatenate([a, gg], axis=1)
    out = jnp.dot(mix, w_ref[...], preferred_element_type=F32).reshape(g, r, D_MODEL)
    xo = x_ref[...] + gate_ref[...] * out
    y = xo * lax.rsqrt(jnp.mean(xo * xo, axis=-1, keepdims=True) + NORM_EPS)
    y_ref[...] = y * nf_ref[...]


def _postmix(a3, g3, x3, mod3, w_out_bf, norm_f, *, g, r):
    n_groups, rows_per_group, _ = x3.shape
    grid = (n_groups // g, rows_per_group // r)
    act = lambda n: pl.BlockSpec((g, r, n), lambda o, i: (o, i, 0))
    return pl.pallas_call(
        functools.partial(_postmix_body, g=g, r=r),
        out_shape=jax.ShapeDtypeStruct(x3.shape, F32),
        grid=grid,
        in_specs=[act(D_ATTN), act(D_CONV), act(D_MODEL),
                  pl.BlockSpec((g, 1, D_MODEL), lambda o, i: (o, 0, 2)),
                  pl.BlockSpec((D_ATTN + D_CONV, D_MODEL), lambda o, i: (0, 0)),
                  pl.BlockSpec((1, 1, D_MODEL), lambda o, i: (0, 0, 0))],
        out_specs=act(D_MODEL),
        compiler_params=pltpu.CompilerParams(dimension_semantics=("arbitrary", "arbitrary"),
                                             vmem_limit_bytes=VMEM_LIMIT),
        name="postmix",
    )(a3, g3, x3, mod3, w_out_bf, norm_f.reshape(1, 1, D_MODEL))


def _lambda(lq1_ref, lk1_ref, lq2_ref, lk2_ref, lam_init):
    l1 = jnp.sum(lq1_ref[...] * lk1_ref[...], axis=1, keepdims=True)
    l2 = jnp.sum(lq2_ref[...] * lk2_ref[...], axis=1, keepdims=True)
    return jnp.exp(l1) - jnp.exp(l2) + lam_init


def _sub_ln_gate(o, subln_ref, sz, lam_init):
    on = o * lax.rsqrt(jnp.mean(o * o, axis=-1, keepdims=True) + SUBLN_EPS)
    return on * (subln_ref[...] * (1.0 - lam_init)) * sz


def _pattn_body(qa_ref, qb_ref, k_ref, v_ref, bias_ref, sz_ref, lq1_ref, lk1_ref, lq2_ref,
                lk2_ref, subln_ref, o_ref, s_scr, m_scr, acc_scr, *, lam_init):
    qi = pl.program_id(2)
    q2 = jnp.concatenate([qa_ref[0], qb_ref[0]], axis=0)
    m_scr[...] = jnp.full(m_scr.shape, NEG_INF, F32)
    acc_scr[...] = jnp.zeros(acc_scr.shape, F32)

    def scores(j):
        return _dot_nt(q2, k_ref[0, pl.ds(pl.multiple_of(j * TK, TK), TK), :])

    def absorb(s, j):
        width = s.shape[1]
        v_t = v_ref[0, pl.ds(pl.multiple_of(j * TK, TK), width), :]
        m_prev = m_scr[...]
        m_new = jnp.maximum(m_prev, jnp.max(s, axis=1, keepdims=True))
        alpha = jnp.exp2(m_prev - m_new)
        p = jnp.exp2(s - _lane_tile(m_new, width // LANES)).astype(BF16)
        v_aug = jnp.concatenate([v_t, jnp.ones((width, LANES), BF16)], axis=1)
        pv = jnp.dot(p, v_aug, preferred_element_type=F32)
        acc_scr[...] = acc_scr[...] * _lane_tile(alpha, 2) + pv
        m_scr[...] = m_new

    n_far = jnp.maximum(qi * (TQ // TK) - 1, 0)
    s_scr[...] = scores(0)

    def far_tiles(j0, count):
        s = s_scr[...]
        for u in range(count):
            s_next = scores(j0 + u + 1)
            absorb(s, j0 + u)
            s = s_next
        s_scr[...] = s

    def far_group(jj, carry):
        far_tiles(jj * FAR_UNROLL, FAR_UNROLL)
        return carry

    def far_single(j, carry):
        far_tiles(j, 1)
        return carry

    n_groups = n_far // FAR_UNROLL
    lax.fori_loop(0, n_groups, far_group, 0)
    lax.fori_loop(n_groups * FAR_UNROLL, n_far, far_single, 0)
    s = s_scr[...]
    for u in range(NEAR_TILES):
        s_next = scores(n_far + u + 1) if u + 1 < NEAR_TILES else None
        absorb(s + _row_tile(bias_ref[0, 0, :, u * TK:(u + 1) * TK]), n_far + u)
        s = s_next

    acc = acc_scr[...]
    o_all = acc[:, :VDIM] / acc[:, VDIM:]
    lam = _lambda(lq1_ref, lk1_ref, lq2_ref, lk2_ref, lam_init)
    o = o_all[:TQ] - lam * o_all[TQ:]
    o_ref[0] = _sub_ln_gate(o, subln_ref, sz_ref[0].astype(F32), lam_init).astype(o_ref.dtype)


def _prompt_attention(qa, qb, kb, vb, bias, sz, lams, subln, *, lam_init):
    b, t, _ = qa.shape
    assert TQ % TK == 0
    qspec = pl.BlockSpec((1, TQ, VDIM), lambda bi, h, qi: (bi, qi, h))
    kvspec = pl.BlockSpec((1, t, VDIM), lambda bi, h, qi: (bi, 0, h))
    small = lambda n: pl.BlockSpec((1, n), lambda bi, h, qi: (0, 0))
    return pl.pallas_call(
        functools.partial(_pattn_body, lam_init=lam_init),
        out_shape=jax.ShapeDtypeStruct((b, t, D_ATTN), BF16),
        grid=(b, N_HEADS, t // TQ),
        in_specs=[qspec, qspec, kvspec, kvspec,
                  pl.BlockSpec((1, 1, TQ, NEAR_TILES * TK),
                               lambda bi, h, qi: (jnp.minimum(qi, 1), h, 0, 0)),
                  qspec, small(HALF_DIM), small(HALF_DIM), small(HALF_DIM), small(HALF_DIM),
                  small(VDIM)],
        out_specs=qspec,
        scratch_shapes=[pltpu.VMEM((2 * TQ, TK), F32), pltpu.VMEM((2 * TQ, LANES), F32),
                        pltpu.VMEM((2 * TQ, 2 * LANES), F32)],
        compiler_params=pltpu.CompilerParams(
            dimension_semantics=("arbitrary", "arbitrary", "arbitrary"),
            vmem_limit_bytes=VMEM_LIMIT),
        name="prompt_attention",
    )(qa, qb, kb, vb, bias, sz, *lams, subln)


def _sattn_body(pt_ref, qa_ref, qb_ref, kn_ref, vn_ref, sz_ref, bias_ref, biasn_ref, lq1_ref,
                lk1_ref, lq2_ref, lk2_ref, subln_ref, *rest, n_pages, n_chunks, lam_init):
    k_refs = rest[:n_pages]
    v_refs = rest[n_pages:2 * n_pages]
    o_ref, m_scr, l_scr, acc_scr = rest[2 * n_pages:]
    c = pl.program_id(1)
    rows = qa_ref.shape[1]

    @pl.when(c == 0)
    def _():
        m_scr[...] = jnp.full(m_scr.shape, NEG_INF, F32)
        l_scr[...] = jnp.zeros(l_scr.shape, F32)
        acc_scr[...] = jnp.zeros(acc_scr.shape, F32)

    def head_slice(h):
        return slice(h * VDIM, (h + 1) * VDIM)

    q_all = jnp.concatenate([r[0, :, head_slice(h)] for h in range(N_HEADS)
                             for r in (qa_ref, qb_ref)], axis=0).astype(BF16)

    def update(s, v_parts):
        width = s.shape[1] // len(v_parts)
        m_prev = m_scr[...]
        m_new = jnp.maximum(m_prev, jnp.max(s, axis=1, keepdims=True))
        alpha = jnp.exp2(m_prev - m_new)
        p = jnp.exp2(s - _lane_tile(m_new, s.shape[1] // LANES))
        l_scr[...] = alpha * l_scr[...] + jnp.sum(p, axis=1, keepdims=True)
        pb = p.astype(BF16)
        pv = None
        for i, v_t in enumerate(v_parts):
            d = jnp.dot(pb[:, i * width:(i + 1) * width], v_t, preferred_element_type=F32)
            pv = d if pv is None else pv + d
        acc_scr[...] = alpha * acc_scr[...] + pv
        m_scr[...] = m_new

    s = jnp.concatenate([_dot_nt(q_all, k_refs[i][...].astype(BF16)) for i in range(n_pages)],
                        axis=1)
    update(s + bias_ref[0], [v_refs[i][...].astype(BF16) for i in range(n_pages)])

    @pl.when(c == n_chunks - 1)
    def _():
        def new_page(ref):
            parts = [ref[0, :, head_slice(h)] for h in range(N_HEADS)]
            parts.append(jnp.zeros((LANES - N_HEADS * rows, VDIM), F32))
            return jnp.concatenate(parts, axis=0).astype(BF16)

        update(_dot_nt(q_all, new_page(kn_ref)) + biasn_ref[...], [new_page(vn_ref)])
        lam = _lambda(lq1_ref, lk1_ref, lq2_ref, lk2_ref, lam_init)
        o_all = acc_scr[...] / l_scr[...]
        for h in range(N_HEADS):
            r0 = 2 * rows * h
            o = o_all[r0:r0 + rows] - lam * o_all[r0 + rows:r0 + 2 * rows]
            o_ref[0, :, head_slice(h)] = _sub_ln_gate(o, subln_ref, sz_ref[0, :, head_slice(h)],
                                                      lam_init)


def _sample_attention(page_table, qa, qb, kn, vn, sz, bias, biasn, lams, subln, cache_k, cache_v,
                      *, lam_init):
    nb, rows, _ = qa.shape
    n_phys, page_rows, _ = cache_k.shape
    n_chunks = page_table.shape[1] // PAGES_PER_STEP
    q_rows = 2 * rows * N_HEADS
    act = pl.BlockSpec((1, rows, D_ATTN), lambda b, c, pt: (b, 0, 0))
    small = lambda n: pl.BlockSpec((1, n), lambda b, c, pt: (0, 0))

    def page_spec(i):
        return pl.BlockSpec((None, page_rows, VDIM),
                            lambda b, c, pt: (pt[b, c * PAGES_PER_STEP + i], 0, 0))

    pages = [page_spec(i) for i in range(PAGES_PER_STEP)]
    grid_spec = pltpu.PrefetchScalarGridSpec(
        num_scalar_prefetch=1,
        grid=(nb, n_chunks),
        in_specs=[act, act, act, act, act,
                  pl.BlockSpec((1, q_rows, PAGES_PER_STEP * page_rows),
                               lambda b, c, pt: (c // (n_chunks - 1), 0, 0)),
                  pl.BlockSpec((q_rows, LANES), lambda b, c, pt: (0, 0)),
                  small(HALF_DIM), small(HALF_DIM), small(HALF_DIM), small(HALF_DIM), small(VDIM)]
                 + pages + pages,
        out_specs=act,
        scratch_shapes=[pltpu.VMEM((q_rows, LANES), F32)] * 3,
    )
    body = functools.partial(_sattn_body, n_pages=PAGES_PER_STEP, n_chunks=n_chunks,
                             lam_init=lam_init)
    return pl.pallas_call(
        body,
        out_shape=jax.ShapeDtypeStruct((nb, rows, D_ATTN), F32),
        grid_spec=grid_spec,
        compiler_params=pltpu.CompilerParams(dimension_semantics=("arbitrary", "arbitrary"),
                                             vmem_limit_bytes=VMEM_LIMIT),
        name="sample_attention",
    )(page_table, qa, qb, kn, vn, sz, bias, biasn, *lams, subln,
      *([cache_k] * PAGES_PER_STEP), *([cache_v] * PAGES_PER_STEP))


def _bias_by_distance(rel_bias, d):
    max_exact = N_BUCKETS // 2
    n = jnp.maximum(d, 0)
    n_f = jnp.maximum(n, 1).astype(F32)
    large = max_exact + (jnp.log(n_f / max_exact) / math.log(MAX_DISTANCE / max_exact)
                         * (N_BUCKETS - max_exact)).astype(jnp.int32)
    bucket = jnp.where(n < max_exact, n, jnp.minimum(large, N_BUCKETS - 1))
    table = rel_bias.astype(F32)
    vals = (table[bucket] - table[N_BUCKETS - 1]).T * LOG2_E
    return jnp.where(d[None, :] >= 0, vals, NEG_INF)


def _toeplitz(rel_bias, delta, n_q, n_k):
    w = n_q + n_k
    row = _bias_by_distance(rel_bias, delta + (n_q - 1) - jnp.arange(w))
    skew = jnp.tile(row, (1, n_q))[:, :n_q * (w - 1)].reshape(N_HEADS, n_q, w - 1)
    return skew[:, :, n_q - 1:n_q - 1 + n_k]


def kernel(x_prompt, x_sample, cache_k, cache_v, state_conv, page_table, c_prompt, c_sample,
           rel_bias, w_ada, b_ada, w_in, w_conv, lambda_q1, lambda_k1, lambda_q2, lambda_k2,
           subln_w, w_out, norm_f):
    depth = w_in.shape[0]
    assert depth == 1
    batch, seq, _ = x_prompt.shape
    nb, dec_seq, _ = x_sample.shape
    n_phys, page = cache_k.shape[1], cache_k.shape[2]
    lam_init = 0.8 - 0.6 * math.exp(-0.3 * 0)

    n_c = batch + nb
    n_c_pad = -(-n_c // MOD_ROWS_PAD) * MOD_ROWS_PAD
    c_all = jnp.concatenate([c_prompt, c_sample, jnp.zeros((n_c_pad - n_c, D_MODEL), F32)], axis=0)
    mod = _modulation(c_all, w_ada[0], b_ada[0].reshape(1, -1))
    mod_p = mod[:batch].reshape(batch, 1, 3 * D_MODEL)
    mod_s = mod[batch:n_c].reshape(nb, 1, 3 * D_MODEL)

    w_in_bf = w_in[0].astype(BF16)
    w_out_bf = w_out[0].astype(BF16)
    lams = [a.reshape(1, HALF_DIM) for a in (lambda_q1[0], lambda_k1[0], lambda_q2[0], lambda_k2[0])]
    subln = subln_w[0].reshape(1, VDIM)

    near = _toeplitz(rel_bias, TK, TQ, TQ + TK)
    first = jnp.concatenate([_toeplitz(rel_bias, 0, TQ, TQ),
                             jnp.full((N_HEADS, TQ, TK), NEG_INF, F32)], axis=2)
    bias_prompt = jnp.stack([first, near])

    q_rows = 2 * dec_seq * N_HEADS
    page_rows = page * N_HEADS
    row_head = jnp.arange(q_rows) // (2 * dec_seq)
    same_head = row_head[:, None] == (jnp.arange(page_rows) % N_HEADS)[None, :]
    head_mask = jnp.where(same_head, 0.0, NEG_INF).astype(F32)
    both_maps = lambda t: jnp.concatenate([t, t], axis=1).reshape(q_rows, t.shape[2])
    t_last = both_maps(_toeplitz(rel_bias, page, dec_seq, page))
    last_page = jnp.where(same_head, jnp.repeat(t_last, N_HEADS, axis=1), NEG_INF)
    bias_s = jnp.stack([jnp.tile(head_mask, (1, PAGES_PER_STEP)),
                        jnp.concatenate([jnp.tile(head_mask, (1, PAGES_PER_STEP - 1)), last_page],
                                        axis=1)])
    t_new = both_maps(_toeplitz(rel_bias, 0, dec_seq, dec_seq))
    new_head = jnp.arange(LANES) // dec_seq
    bias_new = jnp.where(row_head[:, None] == new_head[None, :],
                         jnp.tile(t_new, (1, LANES // dec_seq)), NEG_INF)

    qa, qb, kf, vf, kb, vb, sz, gp, st_p = _premix(
        x_prompt, mod_p, w_in_bf, w_conv[0], jnp.zeros((batch, CONV_WIDTH - 1, D_CONV), F32),
        g=1, r=ROW_TILE, attn_dtype=BF16, emit_bf16_kv=True)
    a_p = _prompt_attention(qa, qb, kb, vb, bias_prompt, sz, lams, subln, lam_init=lam_init)
    y_prompt = _postmix(a_p, gp, x_prompt, mod_p, w_out_bf, norm_f, g=1, r=ROW_TILE)

    g_s = ROW_TILE // dec_seq
    qa_s, qb_s, kf_s, vf_s, sz_s, gs, st_s = _premix(
        x_sample, mod_s, w_in_bf, w_conv[0], state_conv[0],
        g=g_s, r=dec_seq, attn_dtype=F32, emit_bf16_kv=False)
    pages_k = cache_k.reshape(n_phys, page_rows, VDIM)
    pages_v = cache_v.reshape(n_phys, page_rows, VDIM)
    a_s = _sample_attention(page_table, qa_s, qb_s, kf_s, vf_s, sz_s, bias_s, bias_new, lams,
                            subln, pages_k, pages_v, lam_init=lam_init)
    y_sample = _postmix(a_s, gs, x_sample, mod_s, w_out_bf, norm_f, g=g_s, r=dec_seq)

    kv_p = lambda a: a.reshape(1, batch, seq, N_HEADS, VDIM)
    kv_s = lambda a: a.reshape(1, nb, dec_seq, N_HEADS, VDIM)
    return (y_prompt, y_sample, kv_p(kf), kv_p(vf), st_p[None], kv_s(kf_s), kv_s(vf_s), st_s[None])
```

```python
import functools
import math

import jax
import jax.numpy as jnp
from jax import lax
from jax.experimental import pallas as pl
from jax.experimental.pallas import tpu as pltpu

F32 = jnp.float32
BF16 = jnp.bfloat16

D_MODEL = 1024
D_ATTN = 512
D_CONV = 512
HALF_DIM = 64
VDIM = 128
N_HEADS = 4
CONV_WIDTH = 3
N_BUCKETS = 32
MAX_DISTANCE = 128
NORM_EPS = 1e-6
SUBLN_EPS = 1e-5
ATTN_SCALE = HALF_DIM ** -0.5
LOG2_E = math.log2(math.e)
NEG_INF = -1e30
D_IN = 4 * D_ATTN + 4 * D_CONV

LANES = 128
VMEM_LIMIT = 56 * 1024 * 1024

MOD_ROWS_PAD = 8
MOD_TN = 512
ROW_TILE = 512
TQ = 512
TK = 256
NEAR_TILES = TQ // TK + 1
FAR_UNROLL = 4
PAGES_PER_STEP = 16


def _silu(x):
    return x * jax.nn.sigmoid(x)


def _dot_nt(a, b):
    return lax.dot_general(a, b, (((1,), (1,)), ((), ())), preferred_element_type=F32)


def _lane_tile(x, n):
    return x if n == 1 else jnp.concatenate([x] * n, axis=1)


def _row_tile(x):
    return jnp.concatenate([x, x], axis=0)


def _mod_body(c_ref, w_ref, b_ref, o_ref):
    c = c_ref[...]
    o_ref[...] = jnp.dot(_silu(c).astype(BF16), w_ref[...].astype(BF16),
                         preferred_element_type=F32) + b_ref[...]


def _modulation(c_all, w_ada, b_ada):
    rows = c_all.shape[0]
    n = w_ada.shape[1]
    return pl.pallas_call(
        _mod_body,
        out_shape=jax.ShapeDtypeStruct((rows, n), F32),
        grid=(n // MOD_TN,),
        in_specs=[pl.BlockSpec((rows, D_MODEL), lambda j: (0, 0)),
                  pl.BlockSpec((D_MODEL, MOD_TN), lambda j: (0, j)),
                  pl.BlockSpec((1, MOD_TN), lambda j: (0, j))],
        out_specs=pl.BlockSpec((rows, MOD_TN), lambda j: (0, j)),
        compiler_params=pltpu.CompilerParams(dimension_semantics=("arbitrary",),
                                             vmem_limit_bytes=VMEM_LIMIT),
        name="adaln_mod",
    )(c_all, w_ada, b_ada)


def _premix_body(x_ref, shift_ref, scale_ref, w_ref, wconv_ref, prev_ref, *rest, g, r, attn_dtype,
                 emit_bf16_kv):
    if emit_bf16_kv:
        qa_ref, qb_ref, kf_ref, vf_ref, kb_ref, vb_ref, sz_ref, g_ref, st_ref, carry = rest
    else:
        qa_ref, qb_ref, kf_ref, vf_ref, sz_ref, g_ref, st_ref, carry = rest
    rows = g * r

    @pl.when(pl.program_id(1) == 0)
    def _():
        carry[...] = prev_ref[...]

    x = x_ref[...]
    xn = x * lax.rsqrt(jnp.mean(x * x, axis=-1, keepdims=True) + NORM_EPS)
    h = xn * (1.0 + scale_ref[...]) + shift_ref[...]
    h2 = h.reshape(rows, D_MODEL).astype(BF16)

    def proj(i):
        return jnp.dot(h2, w_ref[:, i * D_ATTN:(i + 1) * D_ATTN], preferred_element_type=F32)

    def put(ref, val):
        ref[...] = val.reshape(g, r, val.shape[-1]).astype(ref.dtype)

    q = proj(0) * (ATTN_SCALE * LOG2_E)
    lane = lax.broadcasted_iota(jnp.int32, (rows, D_ATTN), 1)
    first_half = (lane & (VDIM - 1)) < HALF_DIM
    put(qa_ref, jnp.where(first_half, q, 0.0))
    put(qb_ref, jnp.where(first_half, 0.0, q))
    k = proj(1)
    put(kf_ref, k)
    v = proj(2)
    put(vf_ref, v)
    if emit_bf16_kv:
        put(kb_ref, k)
        put(vb_ref, v)
    put(sz_ref, _silu(proj(3)))

    b_gate = proj(4)
    u = proj(5) * proj(6)
    prev = carry[...]
    prev0 = jnp.broadcast_to(prev[:, 0:1, :], (g, r, D_CONV)).reshape(rows, D_CONV)
    prev1 = jnp.broadcast_to(prev[:, 1:2, :], (g, r, D_CONV)).reshape(rows, D_CONV)
    t = lax.broadcasted_iota(jnp.int32, (rows, D_CONV), 0) & (r - 1)
    u1 = jnp.where(t >= 1, pltpu.roll(u, 1, axis=0), prev1)
    u2 = jnp.where(t >= 2, pltpu.roll(u, 2, axis=0), jnp.where(t == 1, prev1, prev0))
    wc = wconv_ref[...]
    conv = wc[0:1, :] * u2 + wc[1:2, :] * u1 + wc[2:3, :] * u
    put(g_ref, b_gate * conv * _silu(proj(7)))

    last2 = u.reshape(g, r, D_CONV)[:, r - 2:, :]
    carry[...] = last2
    st_ref[...] = last2


def _premix(x3, mod3, w_in_bf, w_conv, prev_state, *, g, r, attn_dtype, emit_bf16_kv):
    n_groups, rows_per_group, _ = x3.shape
    grid = (n_groups // g, rows_per_group // r)
    assert r & (r - 1) == 0 and r >= 8
    act = lambda n: pl.BlockSpec((g, r, n), lambda o, i: (o, i, 0))
    mod_spec = lambda col: pl.BlockSpec((g, 1, D_MODEL), lambda o, i: (o, 0, col))
    sds = lambda dt: jax.ShapeDtypeStruct((n_groups, rows_per_group, D_ATTN), dt)
    out_shape = [sds(attn_dtype), sds(attn_dtype), sds(F32), sds(F32)]
    if emit_bf16_kv:
        out_shape += [sds(BF16), sds(BF16)]
    out_shape += [sds(attn_dtype), sds(attn_dtype),
                  jax.ShapeDtypeStruct((n_groups, CONV_WIDTH - 1, D_CONV), F32)]
    out_specs = [act(D_ATTN)] * (len(out_shape) - 1)
    out_specs.append(pl.BlockSpec((g, CONV_WIDTH - 1, D_CONV), lambda o, i: (o, 0, 0)))
    body = functools.partial(_premix_body, g=g, r=r, attn_dtype=attn_dtype,
                             emit_bf16_kv=emit_bf16_kv)
    return pl.pallas_call(
        body,
        out_shape=out_shape,
        grid=grid,
        in_specs=[act(D_MODEL), mod_spec(0), mod_spec(1),
                  pl.BlockSpec((D_MODEL, D_IN), lambda o, i: (0, 0)),
                  pl.BlockSpec((CONV_WIDTH, D_CONV), lambda o, i: (0, 0)),
                  pl.BlockSpec((g, CONV_WIDTH - 1, D_CONV), lambda o, i: (o, 0, 0))],
        out_specs=out_specs,
        scratch_shapes=[pltpu.VMEM((g, CONV_WIDTH - 1, D_CONV), F32)],
        compiler_params=pltpu.CompilerParams(dimension_semantics=("arbitrary", "arbitrary"),
                                             vmem_limit_bytes=VMEM_LIMIT),
        name="premix",
    )(x3, mod3, mod3, w_in_bf, w_conv, prev_state)


def _postmix_body(a_ref, g_ref, x_ref, gate_ref, w_ref, nf_ref, y_ref, *, g, r):
    rows = g * r
    a = a_ref[...].reshape(rows, D_ATTN).astype(BF16)
    gg = g_ref[...].reshape(rows, D_CONV).astype(BF16)
    mix = jnp.concatenate([a, gg], axis=1)
    out = jnp.dot(mix, w_ref[...], preferred_element_type=F32).reshape(g, r, D_MODEL)
    xo = x_ref[...] + gate_ref[...] * out
    y = xo * lax.rsqrt(jnp.mean(xo * xo, axis=-1, keepdims=True) + NORM_EPS)
    y_ref[...] = y * nf_ref[...]


def _postmix(a3, g3, x3, mod3, w_out_bf, norm_f, *, g, r):
    n_groups, rows_per_group, _ = x3.shape
    grid = (n_groups // g, rows_per_group // r)
    act = lambda n: pl.BlockSpec((g, r, n), lambda o, i: (o, i, 0))
    return pl.pallas_call(
        functools.partial(_postmix_body, g=g, r=r),
        out_shape=jax.ShapeDtypeStruct(x3.shape, F32),
        grid=grid,
        in_specs=[act(D_ATTN), act(D_CONV), act(D_MODEL),
                  pl.BlockSpec((g, 1, D_MODEL), lambda o, i: (o, 0, 2)),
                  pl.BlockSpec((D_ATTN + D_CONV, D_MODEL), lambda o, i: (0, 0)),
                  pl.BlockSpec((1, 1, D_MODEL), lambda o, i: (0, 0, 0))],
        out_specs=act(D_MODEL),
        compiler_params=pltpu.CompilerParams(dimension_semantics=("arbitrary", "arbitrary"),
                                             vmem_limit_bytes=VMEM_LIMIT),
        name="postmix",
    )(a3, g3, x3, mod3, w_out_bf, norm_f.reshape(1, 1, D_MODEL))


def _lambda(lq1_ref, lk1_ref, lq2_ref, lk2_ref, lam_init):
    l1 = jnp.sum(lq1_ref[...] * lk1_ref[...], axis=1, keepdims=True)
    l2 = jnp.sum(lq2_ref[...] * lk2_ref[...], axis=1, keepdims=True)
    return jnp.exp(l1) - jnp.exp(l2) + lam_init


def _sub_ln_gate(o, subln_ref, sz, lam_init):
    on = o * lax.rsqrt(jnp.mean(o * o, axis=-1, keepdims=True) + SUBLN_EPS)
    return on * (subln_ref[...] * (1.0 - lam_init)) * sz


def _pattn_body(qa_ref, qb_ref, k_ref, v_ref, bias_ref, sz_ref, lq1_ref, lk1_ref, lq2_ref,
                lk2_ref, subln_ref, o_ref, s_scr, m_scr, acc_scr, *, lam_init):
    qi = pl.program_id(2)
    q2 = jnp.concatenate([qa_ref[0], qb_ref[0]], axis=0)
    m_scr[...] = jnp.full(m_scr.shape, NEG_INF, F32)
    acc_scr[...] = jnp.zeros(acc_scr.shape, F32)

    def scores(j):
        return _dot_nt(q2, k_ref[0, pl.ds(pl.multiple_of(j * TK, TK), TK), :])

    def absorb(s, j):
        width = s.shape[1]
        v_t = v_ref[0, pl.ds(pl.multiple_of(j * TK, TK), width), :]
        m_prev = m_scr[...]
        m_new = jnp.maximum(m_prev, jnp.max(s, axis=1, keepdims=True))
        alpha = jnp.exp2(m_prev - m_new)
        p = jnp.exp2(s - _lane_tile(m_new, width // LANES)).astype(BF16)
        v_aug = jnp.concatenate([v_t, jnp.ones((width, LANES), BF16)], axis=1)
        pv = jnp.dot(p, v_aug, preferred_element_type=F32)
        acc_scr[...] = acc_scr[...] * _lane_tile(alpha, 2) + pv
        m_scr[...] = m_new

    n_far = jnp.maximum(qi * (TQ // TK) - 1, 0)
    s_scr[...] = scores(0)

    def far_tiles(j0, count):
        s = s_scr[...]
        for u in range(count):
            s_next = scores(j0 + u + 1)
            absorb(s, j0 + u)
            s = s_next
        s_scr[...] = s

    def far_group(jj, carry):
        far_tiles(jj * FAR_UNROLL, FAR_UNROLL)
        return carry

    def far_single(j, carry):
        far_tiles(j, 1)
        return carry

    n_groups = n_far // FAR_UNROLL
    lax.fori_loop(0, n_groups, far_group, 0)
    lax.fori_loop(n_groups * FAR_UNROLL, n_far, far_single, 0)
    s = s_scr[...]
    for u in range(NEAR_TILES):
        s_next = scores(n_far + u + 1) if u + 1 < NEAR_TILES else None
        absorb(s + _row_tile(bias_ref[0, 0, :, u * TK:(u + 1) * TK]), n_far + u)
        s = s_next

    acc = acc_scr[...]
    o_all = acc[:, :VDIM] / acc[:, VDIM:]
    lam = _lambda(lq1_ref, lk1_ref, lq2_ref, lk2_ref, lam_init)
    o = o_all[:TQ] - lam * o_all[TQ:]
    o_ref[0] = _sub_ln_gate(o, subln_ref, sz_ref[0].astype(F32), lam_init).astype(o_ref.dtype)


def _prompt_attention(qa, qb, kb, vb, bias, sz, lams, subln, *, lam_init):
    b, t, _ = qa.shape
    assert TQ % TK == 0
    qspec = pl.BlockSpec((1, TQ, VDIM), lambda bi, h, qi: (bi, qi, h))
    kvspec = pl.BlockSpec((1, t, VDIM), lambda bi, h, qi: (bi, 0, h))
    small = lambda n: pl.BlockSpec((1, n), lambda bi, h, qi: (0, 0))
    return pl.pallas_call(
        functools.partial(_pattn_body, lam_init=lam_init),
        out_shape=jax.ShapeDtypeStruct((b, t, D_ATTN), BF16),
        grid=(b, N_HEADS, t // TQ),
        in_specs=[qspec, qspec, kvspec, kvspec,
                  pl.BlockSpec((1, 1, TQ, NEAR_TILES * TK),
                               lambda bi, h, qi: (jnp.minimum(qi, 1), h, 0, 0)),
                  qspec, small(HALF_DIM), small(HALF_DIM), small(HALF_DIM), small(HALF_DIM),
                  small(VDIM)],
        out_specs=qspec,
        scratch_shapes=[pltpu.VMEM((2 * TQ, TK), F32), pltpu.VMEM((2 * TQ, LANES), F32),
                        pltpu.VMEM((2 * TQ, 2 * LANES), F32)],
        compiler_params=pltpu.CompilerParams(
            dimension_semantics=("arbitrary", "arbitrary", "arbitrary"),
            vmem_limit_bytes=VMEM_LIMIT),
        name="prompt_attention",
    )(qa, qb, kb, vb, bias, sz, *lams, subln)


def _sattn_body(pt_ref, qa_ref, qb_ref, kn_ref, vn_ref, sz_ref, bias_ref, biasn_ref, lq1_ref,
                lk1_ref, lq2_ref, lk2_ref, subln_ref, k_hbm, v_hbm, o_ref, k_buf, v_buf, sems,
                m_scr, l_scr, acc_scr, *, n_pages, n_chunks, lam_init):
    c = pl.program_id(1)
    rows = qa_ref.shape[1]
    step = pl.program_id(0) * n_chunks + c
    n_steps = pl.num_programs(0) * n_chunks
    slot = lax.rem(step, 2)

    def page_copies(step_idx, slot_idx):
        copies = []
        for i in range(n_pages):
            page_id = pt_ref[step_idx * n_pages + i]
            copies.append(pltpu.make_async_copy(k_hbm.at[page_id], k_buf.at[slot_idx, i],
                                                sems.at[0, slot_idx]))
            copies.append(pltpu.make_async_copy(v_hbm.at[page_id], v_buf.at[slot_idx, i],
                                                sems.at[1, slot_idx]))
        return copies

    @pl.when(step == 0)
    def _():
        for cp in page_copies(0, 0):
            cp.start()

    @pl.when(step + 1 < n_steps)
    def _():
        for cp in page_copies(step + 1, 1 - slot):
            cp.start()

    @pl.when(c == 0)
    def _():
        m_scr[...] = jnp.full(m_scr.shape, NEG_INF, F32)
        l_scr[...] = jnp.zeros(l_scr.shape, F32)
        acc_scr[...] = jnp.zeros(acc_scr.shape, F32)

    for cp in page_copies(step, slot):
        cp.wait()

    def head_slice(h):
        return slice(h * VDIM, (h + 1) * VDIM)

    q_all = jnp.concatenate([r[0, :, head_slice(h)] for h in range(N_HEADS)
                             for r in (qa_ref, qb_ref)], axis=0).astype(BF16)

    def update(s, v_parts):
        width = s.shape[1] // len(v_parts)
        m_prev = m_scr[...]
        m_new = jnp.maximum(m_prev, jnp.max(s, axis=1, keepdims=True))
        alpha = jnp.exp2(m_prev - m_new)
        p = jnp.exp2(s - _lane_tile(m_new, s.shape[1] // LANES))
        l_scr[...] = alpha * l_scr[...] + jnp.sum(p, axis=1, keepdims=True)
        pb = p.astype(BF16)
        pv = None
        for i, v_t in enumerate(v_parts):
            d = jnp.dot(pb[:, i * width:(i + 1) * width], v_t, preferred_element_type=F32)
            pv = d if pv is None else pv + d
        acc_scr[...] = alpha * acc_scr[...] + pv
        m_scr[...] = m_new

    s = jnp.concatenate([_dot_nt(q_all, k_buf[slot, i].astype(BF16)) for i in range(n_pages)],
                        axis=1)
    update(s + bias_ref[0], [v_buf[slot, i].astype(BF16) for i in range(n_pages)])

    @pl.when(c == n_chunks - 1)
    def _():
        def new_page(ref):
            parts = [ref[0, :, head_slice(h)] for h in range(N_HEADS)]
            parts.append(jnp.zeros((LANES - N_HEADS * rows, VDIM), F32))
            return jnp.concatenate(parts, axis=0).astype(BF16)

        update(_dot_nt(q_all, new_page(kn_ref)) + biasn_ref[...], [new_page(vn_ref)])
        lam = _lambda(lq1_ref, lk1_ref, lq2_ref, lk2_ref, lam_init)
        o_all = acc_scr[...] / l_scr[...]
        for h in range(N_HEADS):
            r0 = 2 * rows * h
            o = o_all[r0:r0 + rows] - lam * o_all[r0 + rows:r0 + 2 * rows]
            o_ref[0, :, head_slice(h)] = _sub_ln_gate(o, subln_ref, sz_ref[0, :, head_slice(h)],
                                                      lam_init)


def _sample_attention(page_table, qa, qb, kn, vn, sz, bias, biasn, lams, subln, cache_k, cache_v,
                      *, lam_init):
    nb, rows, _ = qa.shape
    n_phys, page_rows, _ = cache_k.shape
    n_chunks = page_table.shape[1] // PAGES_PER_STEP
    q_rows = 2 * rows * N_HEADS
    act = pl.BlockSpec((1, rows, D_ATTN), lambda b, c, pt: (b, 0, 0))
    small = lambda n: pl.BlockSpec((1, n), lambda b, c, pt: (0, 0))

    hbm = pl.BlockSpec(memory_space=pl.ANY)
    page_buf = pltpu.VMEM((2, PAGES_PER_STEP, page_rows, VDIM), F32)
    grid_spec = pltpu.PrefetchScalarGridSpec(
        num_scalar_prefetch=1,
        grid=(nb, n_chunks),
        in_specs=[act, act, act, act, act,
                  pl.BlockSpec((1, q_rows, PAGES_PER_STEP * page_rows),
                               lambda b, c, pt: (c // (n_chunks - 1), 0, 0)),
                  pl.BlockSpec((q_rows, LANES), lambda b, c, pt: (0, 0)),
                  small(HALF_DIM), small(HALF_DIM), small(HALF_DIM), small(HALF_DIM), small(VDIM),
                  hbm, hbm],
        out_specs=act,
        scratch_shapes=[page_buf, page_buf, pltpu.SemaphoreType.DMA((2, 2))]
                       + [pltpu.VMEM((q_rows, LANES), F32)] * 3,
    )
    body = functools.partial(_sattn_body, n_pages=PAGES_PER_STEP, n_chunks=n_chunks,
                             lam_init=lam_init)
    return pl.pallas_call(
        body,
        out_shape=jax.ShapeDtypeStruct((nb, rows, D_ATTN), F32),
        grid_spec=grid_spec,
        compiler_params=pltpu.CompilerParams(dimension_semantics=("arbitrary", "arbitrary"),
                                             vmem_limit_bytes=VMEM_LIMIT),
        name="sample_attention",
    )(page_table.reshape(-1), qa, qb, kn, vn, sz, bias, biasn, *lams, subln, cache_k, cache_v)


def _bias_by_distance(rel_bias, d):
    max_exact = N_BUCKETS // 2
    n = jnp.maximum(d, 0)
    n_f = jnp.maximum(n, 1).astype(F32)
    large = max_exact + (jnp.log(n_f / max_exact) / math.log(MAX_DISTANCE / max_exact)
                         * (N_BUCKETS - max_exact)).astype(jnp.int32)
    bucket = jnp.where(n < max_exact, n, jnp.minimum(large, N_BUCKETS - 1))
    table = rel_bias.astype(F32)
    vals = (table[bucket] - table[N_BUCKETS - 1]).T * LOG2_E
    return jnp.where(d[None, :] >= 0, vals, NEG_INF)


def _toeplitz(rel_bias, delta, n_q, n_k):
    w = n_q + n_k
    row = _bias_by_distance(rel_bias, delta + (n_q - 1) - jnp.arange(w))
    skew = jnp.tile(row, (1, n_q))[:, :n_q * (w - 1)].reshape(N_HEADS, n_q, w - 1)
    return skew[:, :, n_q - 1:n_q - 1 + n_k]


def kernel(x_prompt, x_sample, cache_k, cache_v, state_conv, page_table, c_prompt, c_sample,
           rel_bias, w_ada, b_ada, w_in, w_conv, lambda_q1, lambda_k1, lambda_q2, lambda_k2,
           subln_w, w_out, norm_f):
    depth = w_in.shape[0]
    assert depth == 1
    batch, seq, _ = x_prompt.shape
    nb, dec_seq, _ = x_sample.shape
    n_phys, page = cache_k.shape[1], cache_k.shape[2]
    lam_init = 0.8 - 0.6 * math.exp(-0.3 * 0)

    n_c = batch + nb
    n_c_pad = -(-n_c // MOD_ROWS_PAD) * MOD_ROWS_PAD
    c_all = jnp.concatenate([c_prompt, c_sample, jnp.zeros((n_c_pad - n_c, D_MODEL), F32)], axis=0)
    mod = _modulation(c_all, w_ada[0], b_ada[0].reshape(1, -1))
    mod_p = mod[:batch].reshape(batch, 1, 3 * D_MODEL)
    mod_s = mod[batch:n_c].reshape(nb, 1, 3 * D_MODEL)

    w_in_bf = w_in[0].astype(BF16)
    w_out_bf = w_out[0].astype(BF16)
    lams = [a.reshape(1, HALF_DIM) for a in (lambda_q1[0], lambda_k1[0], lambda_q2[0], lambda_k2[0])]
    subln = subln_w[0].reshape(1, VDIM)

    near = _toeplitz(rel_bias, TK, TQ, TQ + TK)
    first = jnp.concatenate([_toeplitz(rel_bias, 0, TQ, TQ),
                             jnp.full((N_HEADS, TQ, TK), NEG_INF, F32)], axis=2)
    bias_prompt = jnp.stack([first, near])

    q_rows = 2 * dec_seq * N_HEADS
    page_rows = page * N_HEADS
    row_head = jnp.arange(q_rows) // (2 * dec_seq)
    same_head = row_head[:, None] == (jnp.arange(page_rows) % N_HEADS)[None, :]
    head_mask = jnp.where(same_head, 0.0, NEG_INF).astype(F32)
    both_maps = lambda t: jnp.concatenate([t, t], axis=1).reshape(q_rows, t.shape[2])
    t_last = both_maps(_toeplitz(rel_bias, page, dec_seq, page))
    last_page = jnp.where(same_head, jnp.repeat(t_last, N_HEADS, axis=1), NEG_INF)
    bias_s = jnp.stack([jnp.tile(head_mask, (1, PAGES_PER_STEP)),
                        jnp.concatenate([jnp.tile(head_mask, (1, PAGES_PER_STEP - 1)), last_page],
                                        axis=1)])
    t_new = both_maps(_toeplitz(rel_bias, 0, dec_seq, dec_seq))
    new_head = jnp.arange(LANES) // dec_seq
    bias_new = jnp.where(row_head[:, None] == new_head[None, :],
                         jnp.tile(t_new, (1, LANES // dec_seq)), NEG_INF)

    qa, qb, kf, vf, kb, vb, sz, gp, st_p = _premix(
        x_prompt, mod_p, w_in_bf, w_conv[0], jnp.zeros((batch, CONV_WIDTH - 1, D_CONV), F32),
        g=1, r=ROW_TILE, attn_dtype=BF16, emit_bf16_kv=True)
    a_p = _prompt_attention(qa, qb, kb, vb, bias_prompt, sz, lams, subln, lam_init=lam_init)
    y_prompt = _postmix(a_p, gp, x_prompt, mod_p, w_out_bf, norm_f, g=1, r=ROW_TILE)

    g_s = ROW_TILE // dec_seq
    qa_s, qb_s, kf_s, vf_s, sz_s, gs, st_s = _premix(
        x_sample, mod_s, w_in_bf, w_conv[0], state_conv[0],
        g=g_s, r=dec_seq, attn_dtype=F32, emit_bf16_kv=False)
    pages_k = cache_k.reshape(n_phys, page_rows, VDIM)
    pages_v = cache_v.reshape(n_phys, page_rows, VDIM)
    a_s = _sample_attention(page_table, qa_s, qb_s, kf_s, vf_s, sz_s, bias_s, bias_new, lams,
                            subln, pages_k, pages_v, lam_init=lam_init)
    y_sample = _postmix(a_s, gs, x_sample, mod_s, w_out_bf, norm_f, g=g_s, r=dec_seq)

    kv_p = lambda a: a.reshape(1, batch, seq, N_HEADS, VDIM)
    kv_s = lambda a: a.reshape(1, nb, dec_seq, N_HEADS, VDIM)
    return (y_prompt, y_sample, kv_p(kf), kv_p(vf), st_p[None], kv_s(kf_s), kv_s(vf_s), st_s[None])
```

```python
import functools
import math

import jax
import jax.numpy as jnp
from jax import lax
from jax.experimental import pallas as pl
from jax.experimental.pallas import tpu as pltpu

F32 = jnp.float32
BF16 = jnp.bfloat16

D_MODEL = 1024
D_ATTN = 512
D_CONV = 512
HALF_DIM = 64
VDIM = 128
N_HEADS = 4
CONV_WIDTH = 3
N_BUCKETS = 32
MAX_DISTANCE = 128
NORM_EPS = 1e-6
SUBLN_EPS = 1e-5
ATTN_SCALE = HALF_DIM ** -0.5
LOG2_E = math.log2(math.e)
NEG_INF = -1e30
D_IN = 4 * D_ATTN + 4 * D_CONV

LANES = 128
VMEM_LIMIT = 56 * 1024 * 1024

MOD_ROWS_PAD = 8
MOD_TN = 512
ROW_TILE = 512
TQ = 512
TK = 256
NEAR_TILES = TQ // TK + 1
FAR_UNROLL = 4
CHUNKS_PER_SAMPLE = 4
PAGE_SLOTS = 3


def _silu(x):
    return x * jax.nn.sigmoid(x)


def _dot_nt(a, b):
    return lax.dot_general(a, b, (((1,), (1,)), ((), ())), preferred_element_type=F32)


def _lane_tile(x, n):
    return x if n == 1 else jnp.concatenate([x] * n, axis=1)


def _row_tile(x):
    return jnp.concatenate([x, x], axis=0)


def _mod_body(c_ref, w_ref, b_ref, o_ref):
    c = c_ref[...]
    o_ref[...] = jnp.dot(_silu(c).astype(BF16), w_ref[...].astype(BF16),
                         preferred_element_type=F32) + b_ref[...]


def _modulation(c_all, w_ada, b_ada):
    rows = c_all.shape[0]
    n = w_ada.shape[1]
    return pl.pallas_call(
        _mod_body,
        out_shape=jax.ShapeDtypeStruct((rows, n), F32),
        grid=(n // MOD_TN,),
        in_specs=[pl.BlockSpec((rows, D_MODEL), lambda j: (0, 0)),
                  pl.BlockSpec((D_MODEL, MOD_TN), lambda j: (0, j)),
                  pl.BlockSpec((1, MOD_TN), lambda j: (0, j))],
        out_specs=pl.BlockSpec((rows, MOD_TN), lambda j: (0, j)),
        compiler_params=pltpu.CompilerParams(dimension_semantics=("arbitrary",),
                                             vmem_limit_bytes=VMEM_LIMIT),
        name="adaln_mod",
    )(c_all, w_ada, b_ada)


def _premix_body(x_ref, shift_ref, scale_ref, w_ref, wconv_ref, prev_ref, *rest, g, r, attn_dtype,
                 emit_bf16_kv):
    if emit_bf16_kv:
        qa_ref, qb_ref, kf_ref, vf_ref, kb_ref, vb_ref, sz_ref, g_ref, st_ref, carry = rest
    else:
        qa_ref, qb_ref, kf_ref, vf_ref, sz_ref, g_ref, st_ref, carry = rest
    rows = g * r

    @pl.when(pl.program_id(1) == 0)
    def _():
        carry[...] = prev_ref[...]

    x = x_ref[...]
    xn = x * lax.rsqrt(jnp.mean(x * x, axis=-1, keepdims=True) + NORM_EPS)
    h = xn * (1.0 + scale_ref[...]) + shift_ref[...]
    h2 = h.reshape(rows, D_MODEL).astype(BF16)

    def proj(i):
        return jnp.dot(h2, w_ref[:, i * D_ATTN:(i + 1) * D_ATTN], preferred_element_type=F32)

    def put(ref, val):
        ref[...] = val.reshape(g, r, val.shape[-1]).astype(ref.dtype)

    q = proj(0) * (ATTN_SCALE * LOG2_E)
    lane = lax.broadcasted_iota(jnp.int32, (rows, D_ATTN), 1)
    first_half = (lane & (VDIM - 1)) < HALF_DIM
    put(qa_ref, jnp.where(first_half, q, 0.0))
    put(qb_ref, jnp.where(first_half, 0.0, q))
    k = proj(1)
    put(kf_ref, k)
    v = proj(2)
    put(vf_ref, v)
    if emit_bf16_kv:
        put(kb_ref, k)
        put(vb_ref, v)
    put(sz_ref, _silu(proj(3)))

    b_gate = proj(4)
    u = proj(5) * proj(6)
    prev = carry[...]
    prev0 = jnp.broadcast_to(prev[:, 0:1, :], (g, r, D_CONV)).reshape(rows, D_CONV)
    prev1 = jnp.broadcast_to(prev[:, 1:2, :], (g, r, D_CONV)).reshape(rows, D_CONV)
    t = lax.broadcasted_iota(jnp.int32, (rows, D_CONV), 0) & (r - 1)
    u1 = jnp.where(t >= 1, pltpu.roll(u, 1, axis=0), prev1)
    u2 = jnp.where(t >= 2, pltpu.roll(u, 2, axis=0), jnp.where(t == 1, prev1, prev0))
    wc = wconv_ref[...]
    conv = wc[0:1, :] * u2 + wc[1:2, :] * u1 + wc[2:3, :] * u
    put(g_ref, b_gate * conv * _silu(proj(7)))

    last2 = u.reshape(g, r, D_CONV)[:, r - 2:, :]
    carry[...] = last2
    st_ref[...] = last2


def _premix(x3, mod3, w_in_bf, w_conv, prev_state, *, g, r, attn_dtype, emit_bf16_kv):
    n_groups, rows_per_group, _ = x3.shape
    grid = (n_groups // g, rows_per_group // r)
    assert r & (r - 1) == 0 and r >= 8
    act = lambda n: pl.BlockSpec((g, r, n), lambda o, i: (o, i, 0))
    mod_spec = lambda col: pl.BlockSpec((g, 1, D_MODEL), lambda o, i: (o, 0, col))
    sds = lambda dt: jax.ShapeDtypeStruct((n_groups, rows_per_group, D_ATTN), dt)
    out_shape = [sds(attn_dtype), sds(attn_dtype), sds(F32), sds(F32)]
    if emit_bf16_kv:
        out_shape += [sds(BF16), sds(BF16)]
    out_shape += [sds(attn_dtype), sds(attn_dtype),
                  jax.ShapeDtypeStruct((n_groups, CONV_WIDTH - 1, D_CONV), F32)]
    out_specs = [act(D_ATTN)] * (len(out_shape) - 1)
    out_specs.append(pl.BlockSpec((g, CONV_WIDTH - 1, D_CONV), lambda o, i: (o, 0, 0)))
    body = functools.partial(_premix_body, g=g, r=r, attn_dtype=attn_dtype,
                             emit_bf16_kv=emit_bf16_kv)
    return pl.pallas_call(
        body,
        out_shape=out_shape,
        grid=grid,
        in_specs=[act(D_MODEL), mod_spec(0), mod_spec(1),
                  pl.BlockSpec((D_MODEL, D_IN), lambda o, i: (0, 0)),
                  pl.BlockSpec((CONV_WIDTH, D_CONV), lambda o, i: (0, 0)),
                  pl.BlockSpec((g, CONV_WIDTH - 1, D_CONV), lambda o, i: (o, 0, 0))],
        out_specs=out_specs,
        scratch_shapes=[pltpu.VMEM((g, CONV_WIDTH - 1, D_CONV), F32)],
        compiler_params=pltpu.CompilerParams(dimension_semantics=("arbitrary", "arbitrary"),
                                             vmem_limit_bytes=VMEM_LIMIT),
        name="premix",
    )(x3, mod3, mod3, w_in_bf, w_conv, prev_state)


def _postmix_body(a_ref, g_ref, x_ref, gate_ref, w_ref, nf_ref, y_ref, *, g, r):
    rows = g * r
    a = a_ref[...].reshape(rows, D_ATTN).astype(BF16)
    gg = g_ref[...].reshape(rows, D_CONV).astype(BF16)
    mix = jnp.concatenate([a, gg], axis=1)
    out = jnp.dot(mix, w_ref[...], preferred_element_type=F32).reshape(g, r, D_MODEL)
    xo = x_ref[...] + gate_ref[...] * out
    y = xo * lax.rsqrt(jnp.mean(xo * xo, axis=-1, keepdims=True) + NORM_EPS)
    y_ref[...] = y * nf_ref[...]


def _postmix(a3, g3, x3, mod3, w_out_bf, norm_f, *, g, r):
    n_groups, rows_per_group, _ = x3.shape
    grid = (n_groups // g, rows_per_group // r)
    act = lambda n: pl.BlockSpec((g, r, n), lambda o, i: (o, i, 0))
    return pl.pallas_call(
        functools.partial(_postmix_body, g=g, r=r),
        out_shape=jax.ShapeDtypeStruct(x3.shape, F32),
        grid=grid,
        in_specs=[act(D_ATTN), act(D_CONV), act(D_MODEL),
                  pl.BlockSpec((g, 1, D_MODEL), lambda o, i: (o, 0, 2)),
                  pl.BlockSpec((D_ATTN + D_CONV, D_MODEL), lambda o, i: (0, 0)),
                  pl.BlockSpec((1, 1, D_MODEL), lambda o, i: (0, 0, 0))],
        out_specs=act(D_MODEL),
        compiler_params=pltpu.CompilerParams(dimension_semantics=("arbitrary", "arbitrary"),
                                             vmem_limit_bytes=VMEM_LIMIT),
        name="postmix",
    )(a3, g3, x3, mod3, w_out_bf, norm_f.reshape(1, 1, D_MODEL))


def _lambda(lq1_ref, lk1_ref, lq2_ref, lk2_ref, lam_init):
    l1 = jnp.sum(lq1_ref[...] * lk1_ref[...], axis=1, keepdims=True)
    l2 = jnp.sum(lq2_ref[...] * lk2_ref[...], axis=1, keepdims=True)
    return jnp.exp(l1) - jnp.exp(l2) + lam_init


def _sub_ln_gate(o, subln_ref, sz, lam_init):
    on = o * lax.rsqrt(jnp.mean(o * o, axis=-1, keepdims=True) + SUBLN_EPS)
    return on * (subln_ref[...] * (1.0 - lam_init)) * sz


def _attn_body(pt_ref,
               qa_ref, qb_ref, k_ref, v_ref, bias_ref, sz_ref,
               sqa_ref, sqb_ref, kn_ref, vn_ref, ssz_ref, hmask_ref, lastb_ref, newb_ref,
               lq1_ref, lk1_ref, lq2_ref, lk2_ref, subln_ref, k_hbm, v_hbm,
               o_ref, so_ref,
               s_scr, m_scr, acc_scr, k_buf, v_buf, sems, sm_scr, sl_scr, sacc_scr,
               *, lam_init):
    qi = pl.program_id(2)
    step = (pl.program_id(0) * pl.num_programs(1) + pl.program_id(1)) * pl.num_programs(2) + qi
    n_steps = pl.num_programs(0) * pl.num_programs(1) * pl.num_programs(2)
    lam = _lambda(lq1_ref, lk1_ref, lq2_ref, lk2_ref, lam_init)

    n_slots, n_pages = k_buf.shape[0], k_buf.shape[1]
    chunks = CHUNKS_PER_SAMPLE
    n_chunks = n_steps * chunks
    rows = sqa_ref.shape[1]

    def page_copies(chunk, slot):
        copies = []
        for i in range(n_pages):
            page_id = pt_ref[chunk * n_pages + i]
            copies.append(pltpu.make_async_copy(k_hbm.at[page_id], k_buf.at[slot, i],
                                                sems.at[0, slot]))
            copies.append(pltpu.make_async_copy(v_hbm.at[page_id], v_buf.at[slot, i],
                                                sems.at[1, slot]))
        return copies

    @pl.when(step == 0)
    def _():
        for ahead in range(n_slots - 1):
            for cp in page_copies(ahead, ahead):
                cp.start()

    def head_slice(h):
        return slice(h * VDIM, (h + 1) * VDIM)

    sq_all = jnp.concatenate([r[0, :, head_slice(h)] for h in range(N_HEADS)
                              for r in (sqa_ref, sqb_ref)], axis=0).astype(BF16)
    sm_scr[...] = jnp.full(sm_scr.shape, NEG_INF, F32)
    sl_scr[...] = jnp.zeros(sl_scr.shape, F32)
    sacc_scr[...] = jnp.zeros(sacc_scr.shape, F32)

    def sample_update(s, v_parts):
        width = s.shape[1] // len(v_parts)
        m_prev = sm_scr[...]
        m_new = jnp.maximum(m_prev, jnp.max(s, axis=1, keepdims=True))
        alpha = jnp.exp2(m_prev - m_new)
        p = jnp.exp2(s - _lane_tile(m_new, s.shape[1] // LANES))
        sl_scr[...] = alpha * sl_scr[...] + jnp.sum(p, axis=1, keepdims=True)
        pb = p.astype(BF16)
        pv = None
        for i, v_t in enumerate(v_parts):
            d = jnp.dot(pb[:, i * width:(i + 1) * width], v_t, preferred_element_type=F32)
            pv = d if pv is None else pv + d
        sacc_scr[...] = alpha * sacc_scr[...] + pv
        sm_scr[...] = m_new

    def sample_chunk(j):
        chunk = step * chunks + j
        slot = lax.rem(chunk, n_slots)
        ahead = chunk + (n_slots - 1)

        @pl.when(ahead < n_chunks)
        def _():
            for cp in page_copies(ahead, lax.rem(ahead, n_slots)):
                cp.start()

        for cp in page_copies(chunk, slot):
            cp.wait()
        s = jnp.concatenate([_dot_nt(sq_all, k_buf[slot, i].astype(BF16))
                             for i in range(n_pages)], axis=1)
        hmask = hmask_ref[...]
        last = jnp.where(j == chunks - 1, lastb_ref[...], hmask)
        s = s + jnp.concatenate([hmask] * (n_pages - 1) + [last], axis=1)
        sample_update(s, [v_buf[slot, i].astype(BF16) for i in range(n_pages)])

    q2 = jnp.concatenate([qa_ref[0], qb_ref[0]], axis=0)
    m_scr[...] = jnp.full(m_scr.shape, NEG_INF, F32)
    acc_scr[...] = jnp.zeros(acc_scr.shape, F32)

    def scores(j):
        return _dot_nt(q2, k_ref[0, pl.ds(pl.multiple_of(j * TK, TK), TK), :])

    def absorb(s, j):
        v_t = v_ref[0, pl.ds(pl.multiple_of(j * TK, TK), TK), :]
        m_prev = m_scr[...]
        m_new = jnp.maximum(m_prev, jnp.max(s, axis=1, keepdims=True))
        alpha = jnp.exp2(m_prev - m_new)
        p = jnp.exp2(s - _lane_tile(m_new, TK // LANES)).astype(BF16)
        v_aug = jnp.concatenate([v_t, jnp.ones((TK, LANES), BF16)], axis=1)
        pv = jnp.dot(p, v_aug, preferred_element_type=F32)
        acc_scr[...] = acc_scr[...] * _lane_tile(alpha, 2) + pv
        m_scr[...] = m_new

    n_far = jnp.maximum(qi * (TQ // TK) - 1, 0)
    s_scr[...] = scores(0)

    def far_tiles(j0, count):
        s = s_scr[...]
        for u in range(count):
            s_next = scores(j0 + u + 1)
            absorb(s, j0 + u)
            s = s_next
        s_scr[...] = s

    def far_group(jj, carry):
        far_tiles(jj * FAR_UNROLL, FAR_UNROLL)
        return carry

    def far_single(j, carry):
        far_tiles(j, 1)
        return carry

    n_groups = n_far // FAR_UNROLL

    def interleaved(j, carry):
        sample_chunk(j)
        lax.fori_loop((n_groups * j) // chunks, (n_groups * (j + 1)) // chunks, far_group, 0)
        return carry

    lax.fori_loop(0, chunks, interleaved, 0)
    lax.fori_loop(n_groups * FAR_UNROLL, n_far, far_single, 0)
    s = s_scr[...]
    for u in range(NEAR_TILES):
        s_next = scores(n_far + u + 1) if u + 1 < NEAR_TILES else None
        absorb(s + _row_tile(bias_ref[0, 0, :, u * TK:(u + 1) * TK]), n_far + u)
        s = s_next

    acc = acc_scr[...]
    o_all = acc[:, :VDIM] / acc[:, VDIM:]
    o = o_all[:TQ] - lam * o_all[TQ:]
    o_ref[0] = _sub_ln_gate(o, subln_ref, sz_ref[0].astype(F32), lam_init).astype(o_ref.dtype)

    def new_page(ref):
        parts = [ref[0, :, head_slice(h)] for h in range(N_HEADS)]
        parts.append(jnp.zeros((LANES - N_HEADS * rows, VDIM), F32))
        return jnp.concatenate(parts, axis=0).astype(BF16)

    sample_update(_dot_nt(sq_all, new_page(kn_ref)) + newb_ref[...], [new_page(vn_ref)])
    so_all = sacc_scr[...] / sl_scr[...]
    for h in range(N_HEADS):
        r0 = 2 * rows * h
        so = so_all[r0:r0 + rows] - lam * so_all[r0 + rows:r0 + 2 * rows]
        so_ref[0, :, head_slice(h)] = _sub_ln_gate(so, subln_ref, ssz_ref[0, :, head_slice(h)],
                                                   lam_init)


def _attention(page_table, qa, qb, kb, vb, bias, sz, sqa, sqb, kn, vn, ssz, hmask, lastb, newb,
               lams, subln, cache_k, cache_v, *, lam_init):
    b, t, _ = qa.shape
    nb, rows, _ = sqa.shape
    n_phys, page_rows, _ = cache_k.shape
    nq = t // TQ
    q_rows = 2 * rows * N_HEADS
    pages_per_chunk = page_table.shape[1] // CHUNKS_PER_SAMPLE
    assert TQ % TK == 0 and t % TQ == 0
    assert b * N_HEADS * nq == nb, "one sample per prompt query tile"
    assert page_table.shape[1] % CHUNKS_PER_SAMPLE == 0

    def flat(bi, h, qi):
        return (bi * N_HEADS + h) * nq + qi

    qspec = pl.BlockSpec((1, TQ, VDIM), lambda bi, h, qi, pt: (bi, qi, h))
    kvspec = pl.BlockSpec((1, t, VDIM), lambda bi, h, qi, pt: (bi, 0, h))
    sspec = pl.BlockSpec((1, rows, D_ATTN), lambda bi, h, qi, pt: (flat(bi, h, qi), 0, 0))
    const = lambda *shape: pl.BlockSpec(shape, lambda bi, h, qi, pt: (0,) * len(shape))
    hbm = pl.BlockSpec(memory_space=pl.ANY)
    page_buf = pltpu.VMEM((PAGE_SLOTS, pages_per_chunk, page_rows, VDIM), F32)
    grid_spec = pltpu.PrefetchScalarGridSpec(
        num_scalar_prefetch=1,
        grid=(b, N_HEADS, nq),
        in_specs=[qspec, qspec, kvspec, kvspec,
                  pl.BlockSpec((1, 1, TQ, NEAR_TILES * TK),
                               lambda bi, h, qi, pt: (jnp.minimum(qi, 1), h, 0, 0)),
                  qspec,
                  sspec, sspec, sspec, sspec, sspec,
                  const(q_rows, page_rows), const(q_rows, page_rows), const(q_rows, LANES),
                  const(1, HALF_DIM), const(1, HALF_DIM), const(1, HALF_DIM), const(1, HALF_DIM),
                  const(1, VDIM), hbm, hbm],
        out_specs=[qspec, sspec],
        scratch_shapes=[pltpu.VMEM((2 * TQ, TK), F32), pltpu.VMEM((2 * TQ, LANES), F32),
                        pltpu.VMEM((2 * TQ, 2 * LANES), F32),
                        page_buf, page_buf, pltpu.SemaphoreType.DMA((2, PAGE_SLOTS))]
                       + [pltpu.VMEM((q_rows, LANES), F32)] * 3,
    )
    return pl.pallas_call(
        functools.partial(_attn_body, lam_init=lam_init),
        out_shape=[jax.ShapeDtypeStruct((b, t, D_ATTN), BF16),
                   jax.ShapeDtypeStruct((nb, rows, D_ATTN), F32)],
        grid_spec=grid_spec,
        compiler_params=pltpu.CompilerParams(
            dimension_semantics=("arbitrary", "arbitrary", "arbitrary"),
            vmem_limit_bytes=VMEM_LIMIT),
        name="attention",
    )(page_table.reshape(-1), qa, qb, kb, vb, bias, sz, sqa, sqb, kn, vn, ssz, hmask, lastb, newb,
      *lams, subln, cache_k, cache_v)


def _bias_by_distance(rel_bias, d):
    max_exact = N_BUCKETS // 2
    n = jnp.maximum(d, 0)
    n_f = jnp.maximum(n, 1).astype(F32)
    large = max_exact + (jnp.log(n_f / max_exact) / math.log(MAX_DISTANCE / max_exact)
                         * (N_BUCKETS - max_exact)).astype(jnp.int32)
    bucket = jnp.where(n < max_exact, n, jnp.minimum(large, N_BUCKETS - 1))
    table = rel_bias.astype(F32)
    vals = (table[bucket] - table[N_BUCKETS - 1]).T * LOG2_E
    return jnp.where(d[None, :] >= 0, vals, NEG_INF)


def _toeplitz(rel_bias, delta, n_q, n_k):
    w = n_q + n_k
    row = _bias_by_distance(rel_bias, delta + (n_q - 1) - jnp.arange(w))
    skew = jnp.tile(row, (1, n_q))[:, :n_q * (w - 1)].reshape(N_HEADS, n_q, w - 1)
    return skew[:, :, n_q - 1:n_q - 1 + n_k]


def kernel(x_prompt, x_sample, cache_k, cache_v, state_conv, page_table, c_prompt, c_sample,
           rel_bias, w_ada, b_ada, w_in, w_conv, lambda_q1, lambda_k1, lambda_q2, lambda_k2,
           subln_w, w_out, norm_f):
    depth = w_in.shape[0]
    assert depth == 1
    batch, seq, _ = x_prompt.shape
    nb, dec_seq, _ = x_sample.shape
    n_phys, page = cache_k.shape[1], cache_k.shape[2]
    lam_init = 0.8 - 0.6 * math.exp(-0.3 * 0)

    n_c = batch + nb
    n_c_pad = -(-n_c // MOD_ROWS_PAD) * MOD_ROWS_PAD
    c_all = jnp.concatenate([c_prompt, c_sample, jnp.zeros((n_c_pad - n_c, D_MODEL), F32)], axis=0)
    mod = _modulation(c_all, w_ada[0], b_ada[0].reshape(1, -1))
    mod_p = mod[:batch].reshape(batch, 1, 3 * D_MODEL)
    mod_s = mod[batch:n_c].reshape(nb, 1, 3 * D_MODEL)

    w_in_bf = w_in[0].astype(BF16)
    w_out_bf = w_out[0].astype(BF16)
    lams = [a.reshape(1, HALF_DIM) for a in (lambda_q1[0], lambda_k1[0], lambda_q2[0], lambda_k2[0])]
    subln = subln_w[0].reshape(1, VDIM)

    near = _toeplitz(rel_bias, TK, TQ, TQ + TK)
    first = jnp.concatenate([_toeplitz(rel_bias, 0, TQ, TQ),
                             jnp.full((N_HEADS, TQ, TK), NEG_INF, F32)], axis=2)
    bias_prompt = jnp.stack([first, near])

    q_rows = 2 * dec_seq * N_HEADS
    page_rows = page * N_HEADS
    row_head = jnp.arange(q_rows) // (2 * dec_seq)
    same_head = row_head[:, None] == (jnp.arange(page_rows) % N_HEADS)[None, :]
    head_mask = jnp.where(same_head, 0.0, NEG_INF).astype(F32)
    both_maps = lambda t: jnp.concatenate([t, t], axis=1).reshape(q_rows, t.shape[2])
    t_last = both_maps(_toeplitz(rel_bias, page, dec_seq, page))
    last_page = jnp.where(same_head, jnp.repeat(t_last, N_HEADS, axis=1), NEG_INF)
    t_new = both_maps(_toeplitz(rel_bias, 0, dec_seq, dec_seq))
    new_head = jnp.arange(LANES) // dec_seq
    bias_new = jnp.where(row_head[:, None] == new_head[None, :],
                         jnp.tile(t_new, (1, LANES // dec_seq)), NEG_INF)

    qa, qb, kf, vf, kb, vb, sz, gp, st_p = _premix(
        x_prompt, mod_p, w_in_bf, w_conv[0], jnp.zeros((batch, CONV_WIDTH - 1, D_CONV), F32),
        g=1, r=ROW_TILE, attn_dtype=BF16, emit_bf16_kv=True)
    g_s = ROW_TILE // dec_seq
    qa_s, qb_s, kf_s, vf_s, sz_s, gs, st_s = _premix(
        x_sample, mod_s, w_in_bf, w_conv[0], state_conv[0],
        g=g_s, r=dec_seq, attn_dtype=F32, emit_bf16_kv=False)
    pages_k = cache_k.reshape(n_phys, page_rows, VDIM)
    pages_v = cache_v.reshape(n_phys, page_rows, VDIM)
    a_p, a_s = _attention(page_table, qa, qb, kb, vb, bias_prompt, sz, qa_s, qb_s, kf_s, vf_s, sz_s,
                          head_mask, last_page, bias_new, lams, subln, pages_k, pages_v,
                          lam_init=lam_init)
    y_prompt = _postmix(a_p, gp, x_prompt, mod_p, w_out_bf, norm_f, g=1, r=ROW_TILE)
    y_sample = _postmix(a_s, gs, x_sample, mod_s, w_out_bf, norm_f, g=g_s, r=dec_seq)

    kv_p = lambda a: a.reshape(1, batch, seq, N_HEADS, VDIM)
    kv_s = lambda a: a.reshape(1, nb, dec_seq, N_HEADS, VDIM)
    return (y_prompt, y_sample, kv_p(kf), kv_p(vf), st_p[None], kv_s(kf_s), kv_s(vf_s), st_s[None])
```

```python
import functools
import math

import jax
import jax.numpy as jnp
from jax import lax
from jax.experimental import pallas as pl
from jax.experimental.pallas import tpu as pltpu

F32 = jnp.float32
BF16 = jnp.bfloat16

D_MODEL = 1024
D_ATTN = 512
D_CONV = 512
HALF_DIM = 64
VDIM = 128
N_HEADS = 4
CONV_WIDTH = 3
N_BUCKETS = 32
MAX_DISTANCE = 128
NORM_EPS = 1e-6
SUBLN_EPS = 1e-5
ATTN_SCALE = HALF_DIM ** -0.5
LOG2_E = math.log2(math.e)
NEG_INF = -1e30
D_IN = 4 * D_ATTN + 4 * D_CONV

LANES = 128
VMEM_LIMIT = 56 * 1024 * 1024

MOD_ROWS_PAD = 8
MOD_TN = 512
ROW_TILE = 512
TQ = 512
TK = 256
NEAR_TILES = TQ // TK + 1
FAR_UNROLL = 4
CHUNKS_PER_SAMPLE = 4
PAGE_SLOTS = 3


def _silu(x):
    return x * jax.nn.sigmoid(x)


def _dot_nt(a, b):
    return lax.dot_general(a, b, (((1,), (1,)), ((), ())), preferred_element_type=F32)


def _lane_tile(x, n):
    return x if n == 1 else jnp.concatenate([x] * n, axis=1)


def _row_tile(x):
    return jnp.concatenate([x, x], axis=0)


def _mod_body(c_ref, w_ref, b_ref, o_ref):
    c = c_ref[...]
    o_ref[...] = jnp.dot(_silu(c).astype(BF16), w_ref[...].astype(BF16),
                         preferred_element_type=F32) + b_ref[...]


def _modulation(c_all, w_ada, b_ada):
    rows = c_all.shape[0]
    n = w_ada.shape[1]
    return pl.pallas_call(
        _mod_body,
        out_shape=jax.ShapeDtypeStruct((rows, n), F32),
        grid=(n // MOD_TN,),
        in_specs=[pl.BlockSpec((rows, D_MODEL), lambda j: (0, 0)),
                  pl.BlockSpec((D_MODEL, MOD_TN), lambda j: (0, j)),
                  pl.BlockSpec((1, MOD_TN), lambda j: (0, j))],
        out_specs=pl.BlockSpec((rows, MOD_TN), lambda j: (0, j)),
        compiler_params=pltpu.CompilerParams(dimension_semantics=("arbitrary",),
                                             vmem_limit_bytes=VMEM_LIMIT),
        name="adaln_mod",
    )(c_all, w_ada, b_ada)


def _premix_body(x_ref, shift_ref, scale_ref, w_ref, wconv_ref, prev_ref, *rest, g, r, attn_dtype,
                 emit_bf16_kv):
    if emit_bf16_kv:
        qa_ref, qb_ref, kf_ref, vf_ref, kb_ref, vb_ref, sz_ref, g_ref, st_ref, carry = rest
    else:
        qa_ref, qb_ref, kf_ref, vf_ref, sz_ref, g_ref, st_ref, carry = rest
    rows = g * r

    @pl.when(pl.program_id(1) == 0)
    def _():
        carry[...] = prev_ref[...]

    x = x_ref[...]
    xn = x * lax.rsqrt(jnp.mean(x * x, axis=-1, keepdims=True) + NORM_EPS)
    h = xn * (1.0 + scale_ref[...]) + shift_ref[...]
    h2 = h.reshape(rows, D_MODEL).astype(BF16)

    def proj(i):
        return jnp.dot(h2, w_ref[:, i * D_ATTN:(i + 1) * D_ATTN], preferred_element_type=F32)

    def put(ref, val):
        ref[...] = val.reshape(g, r, val.shape[-1]).astype(ref.dtype)

    q = proj(0) * (ATTN_SCALE * LOG2_E)
    lane = lax.broadcasted_iota(jnp.int32, (rows, D_ATTN), 1)
    first_half = (lane & (VDIM - 1)) < HALF_DIM
    put(qa_ref, jnp.where(first_half, q, 0.0))
    put(qb_ref, jnp.where(first_half, 0.0, q))
    def put_cache_rows(ref, val):
        val3 = val.reshape(g, r, D_ATTN)
        for hd in range(N_HEADS):
            ref[:, pl.ds(hd, r, stride=N_HEADS), :] = val3[:, :, hd * VDIM:(hd + 1) * VDIM]

    k = proj(1)
    put_cache_rows(kf_ref, k)
    v = proj(2)
    put_cache_rows(vf_ref, v)
    if emit_bf16_kv:
        put(kb_ref, k)
        put(vb_ref, v)
    put(sz_ref, _silu(proj(3)))

    b_gate = proj(4)
    u = proj(5) * proj(6)
    prev = carry[...]
    prev0 = jnp.broadcast_to(prev[:, 0:1, :], (g, r, D_CONV)).reshape(rows, D_CONV)
    prev1 = jnp.broadcast_to(prev[:, 1:2, :], (g, r, D_CONV)).reshape(rows, D_CONV)
    t = lax.broadcasted_iota(jnp.int32, (rows, D_CONV), 0) & (r - 1)
    u1 = jnp.where(t >= 1, pltpu.roll(u, 1, axis=0), prev1)
    u2 = jnp.where(t >= 2, pltpu.roll(u, 2, axis=0), jnp.where(t == 1, prev1, prev0))
    wc = wconv_ref[...]
    conv = wc[0:1, :] * u2 + wc[1:2, :] * u1 + wc[2:3, :] * u
    put(g_ref, b_gate * conv * _silu(proj(7)))

    last2 = u.reshape(g, r, D_CONV)[:, r - 2:, :]
    carry[...] = last2
    st_ref[...] = last2


def _premix(x3, mod3, w_in_bf, w_conv, prev_state, *, g, r, attn_dtype, emit_bf16_kv):
    n_groups, rows_per_group, _ = x3.shape
    grid = (n_groups // g, rows_per_group // r)
    assert r & (r - 1) == 0 and r >= 8
    act = lambda n: pl.BlockSpec((g, r, n), lambda o, i: (o, i, 0))
    mod_spec = lambda col: pl.BlockSpec((g, 1, D_MODEL), lambda o, i: (o, 0, col))
    sds = lambda dt: jax.ShapeDtypeStruct((n_groups, rows_per_group, D_ATTN), dt)
    cache_rows = jax.ShapeDtypeStruct((n_groups, rows_per_group * N_HEADS, VDIM), F32)
    out_shape = [sds(attn_dtype), sds(attn_dtype), cache_rows, cache_rows]
    if emit_bf16_kv:
        out_shape += [sds(BF16), sds(BF16)]
    out_shape += [sds(attn_dtype), sds(attn_dtype),
                  jax.ShapeDtypeStruct((n_groups, CONV_WIDTH - 1, D_CONV), F32)]
    out_specs = [act(D_ATTN)] * (len(out_shape) - 1)
    out_specs[2] = out_specs[3] = pl.BlockSpec((g, r * N_HEADS, VDIM), lambda o, i: (o, i, 0))
    out_specs.append(pl.BlockSpec((g, CONV_WIDTH - 1, D_CONV), lambda o, i: (o, 0, 0)))
    body = functools.partial(_premix_body, g=g, r=r, attn_dtype=attn_dtype,
                             emit_bf16_kv=emit_bf16_kv)
    return pl.pallas_call(
        body,
        out_shape=out_shape,
        grid=grid,
        in_specs=[act(D_MODEL), mod_spec(0), mod_spec(1),
                  pl.BlockSpec((D_MODEL, D_IN), lambda o, i: (0, 0)),
                  pl.BlockSpec((CONV_WIDTH, D_CONV), lambda o, i: (0, 0)),
                  pl.BlockSpec((g, CONV_WIDTH - 1, D_CONV), lambda o, i: (o, 0, 0))],
        out_specs=out_specs,
        scratch_shapes=[pltpu.VMEM((g, CONV_WIDTH - 1, D_CONV), F32)],
        compiler_params=pltpu.CompilerParams(dimension_semantics=("arbitrary", "arbitrary"),
                                             vmem_limit_bytes=VMEM_LIMIT),
        name="premix",
    )(x3, mod3, mod3, w_in_bf, w_conv, prev_state)


def _postmix_body(a_ref, g_ref, x_ref, gate_ref, w_ref, nf_ref, y_ref, *, g, r):
    rows = g * r
    a = a_ref[...].reshape(rows, D_ATTN).astype(BF16)
    gg = g_ref[...].reshape(rows, D_CONV).astype(BF16)
    mix = jnp.concatenate([a, gg], axis=1)
    out = jnp.dot(mix, w_ref[...], preferred_element_type=F32).reshape(g, r, D_MODEL)
    xo = x_ref[...] + gate_ref[...] * out
    y = xo * lax.rsqrt(jnp.mean(xo * xo, axis=-1, keepdims=True) + NORM_EPS)
    y_ref[...] = y * nf_ref[...]


def _postmix(a3, g3, x3, mod3, w_out_bf, norm_f, *, g, r):
    n_groups, rows_per_group, _ = x3.shape
    grid = (n_groups // g, rows_per_group // r)
    act = lambda n: pl.BlockSpec((g, r, n), lambda o, i: (o, i, 0))
    return pl.pallas_call(
        functools.partial(_postmix_body, g=g, r=r),
        out_shape=jax.ShapeDtypeStruct(x3.shape, F32),
        grid=grid,
        in_specs=[act(D_ATTN), act(D_CONV), act(D_MODEL),
                  pl.BlockSpec((g, 1, D_MODEL), lambda o, i: (o, 0, 2)),
                  pl.BlockSpec((D_ATTN + D_CONV, D_MODEL), lambda o, i: (0, 0)),
                  pl.BlockSpec((1, 1, D_MODEL), lambda o, i: (0, 0, 0))],
        out_specs=act(D_MODEL),
        compiler_params=pltpu.CompilerParams(dimension_semantics=("arbitrary", "arbitrary"),
                                             vmem_limit_bytes=VMEM_LIMIT),
        name="postmix",
    )(a3, g3, x3, mod3, w_out_bf, norm_f.reshape(1, 1, D_MODEL))


def _lambda(lq1_ref, lk1_ref, lq2_ref, lk2_ref, lam_init):
    l1 = jnp.sum(lq1_ref[...] * lk1_ref[...], axis=1, keepdims=True)
    l2 = jnp.sum(lq2_ref[...] * lk2_ref[...], axis=1, keepdims=True)
    return jnp.exp(l1) - jnp.exp(l2) + lam_init


def _sub_ln_gate(o, subln_ref, sz, lam_init):
    on = o * lax.rsqrt(jnp.mean(o * o, axis=-1, keepdims=True) + SUBLN_EPS)
    return on * (subln_ref[...] * (1.0 - lam_init)) * sz


def _attn_body(pt_ref,
               qa_ref, qb_ref, k_ref, v_ref, bias_ref, sz_ref,
               sqa_ref, sqb_ref, kn_ref, vn_ref, ssz_ref, hmask_ref, lastb_ref, newb_ref,
               lq1_ref, lk1_ref, lq2_ref, lk2_ref, subln_ref, k_hbm, v_hbm,
               o_ref, so_ref,
               s_scr, m_scr, acc_scr, bias_scr, k_buf, v_buf, sems, sm_scr, sl_scr, sacc_scr,
               *, lam_init):
    qi = pl.program_id(2)
    step = (pl.program_id(0) * pl.num_programs(1) + pl.program_id(1)) * pl.num_programs(2) + qi
    n_steps = pl.num_programs(0) * pl.num_programs(1) * pl.num_programs(2)
    lam = _lambda(lq1_ref, lk1_ref, lq2_ref, lk2_ref, lam_init)

    n_slots, n_pages = k_buf.shape[0], k_buf.shape[1]
    chunks = CHUNKS_PER_SAMPLE
    n_chunks = n_steps * chunks
    rows = sqa_ref.shape[1]

    def page_copies(chunk, slot):
        copies = []
        for i in range(n_pages):
            page_id = pt_ref[chunk * n_pages + i]
            copies.append(pltpu.make_async_copy(k_hbm.at[page_id], k_buf.at[slot, i],
                                                sems.at[0, slot]))
            copies.append(pltpu.make_async_copy(v_hbm.at[page_id], v_buf.at[slot, i],
                                                sems.at[1, slot]))
        return copies

    @pl.when(step == 0)
    def _():
        for ahead in range(n_slots - 1):
            for cp in page_copies(ahead, ahead):
                cp.start()

    def head_slice(h):
        return slice(h * VDIM, (h + 1) * VDIM)

    sq_all = jnp.concatenate([r[0, :, head_slice(h)] for h in range(N_HEADS)
                              for r in (sqa_ref, sqb_ref)], axis=0).astype(BF16)
    sm_scr[...] = jnp.full(sm_scr.shape, NEG_INF, F32)
    sl_scr[...] = jnp.zeros(sl_scr.shape, F32)
    sacc_scr[...] = jnp.zeros(sacc_scr.shape, F32)

    def sample_update(s, v_parts):
        width = s.shape[1] // len(v_parts)
        m_prev = sm_scr[...]
        m_new = jnp.maximum(m_prev, jnp.max(s, axis=1, keepdims=True))
        alpha = jnp.exp2(m_prev - m_new)
        p = jnp.exp2(s - _lane_tile(m_new, s.shape[1] // LANES))
        sl_scr[...] = alpha * sl_scr[...] + jnp.sum(p, axis=1, keepdims=True)
        pb = p.astype(BF16)
        pv = None
        for i, v_t in enumerate(v_parts):
            d = jnp.dot(pb[:, i * width:(i + 1) * width], v_t, preferred_element_type=F32)
            pv = d if pv is None else pv + d
        sacc_scr[...] = alpha * sacc_scr[...] + pv
        sm_scr[...] = m_new

    def sample_chunk(j):
        chunk = step * chunks + j
        slot = lax.rem(chunk, n_slots)
        ahead = chunk + (n_slots - 1)

        @pl.when(ahead < n_chunks)
        def _():
            for cp in page_copies(ahead, lax.rem(ahead, n_slots)):
                cp.start()

        for cp in page_copies(chunk, slot):
            cp.wait()
        s = jnp.concatenate([_dot_nt(sq_all, k_buf[slot, i].astype(BF16))
                             for i in range(n_pages)], axis=1)
        hmask = hmask_ref[...]
        last = jnp.where(j == chunks - 1, lastb_ref[...], hmask)
        s = s + jnp.concatenate([hmask] * (n_pages - 1) + [last], axis=1)
        sample_update(s, [v_buf[slot, i].astype(BF16) for i in range(n_pages)])

    q2 = jnp.concatenate([qa_ref[0], qb_ref[0]], axis=0)
    m_scr[...] = jnp.full(m_scr.shape, NEG_INF, F32)
    acc_scr[...] = jnp.zeros(acc_scr.shape, F32)

    def scores(j):
        return _dot_nt(q2, k_ref[0, pl.ds(pl.multiple_of(j * TK, TK), TK), :])

    def absorb(s, j):
        v_t = v_ref[0, pl.ds(pl.multiple_of(j * TK, TK), TK), :]
        m_prev = m_scr[...]
        m_new = jnp.maximum(m_prev, jnp.max(s, axis=1, keepdims=True))
        alpha = jnp.exp2(m_prev - m_new)
        p = jnp.exp2(s - _lane_tile(m_new, TK // LANES)).astype(BF16)
        v_aug = jnp.concatenate([v_t, jnp.ones((TK, LANES), BF16)], axis=1)
        pv = jnp.dot(p, v_aug, preferred_element_type=F32)
        acc_scr[...] = acc_scr[...] * _lane_tile(alpha, 2) + pv
        m_scr[...] = m_new

    n_far = jnp.maximum(qi * (TQ // TK) - 1, 0)
    s_scr[...] = scores(0)

    @pl.when(qi <= 1)
    def _():
        row = jnp.where(qi == 0, bias_ref[0, 0:1, :], bias_ref[0, 1:2, :])
        skewed = pltpu.roll(jnp.broadcast_to(row, (TQ, row.shape[1])), 0, 1,
                            stride=1, stride_axis=0)
        bias_scr[...] = skewed[:, TQ:TQ + NEAR_TILES * TK]

    def far_tiles(j0, count):
        s = s_scr[...]
        for u in range(count):
            s_next = scores(j0 + u + 1)
            absorb(s, j0 + u)
            s = s_next
        s_scr[...] = s

    def far_group(jj, carry):
        far_tiles(jj * FAR_UNROLL, FAR_UNROLL)
        return carry

    def far_single(j, carry):
        far_tiles(j, 1)
        return carry

    n_groups = n_far // FAR_UNROLL

    def interleaved(j, carry):
        sample_chunk(j)
        lax.fori_loop((n_groups * j) // chunks, (n_groups * (j + 1)) // chunks, far_group, 0)
        return carry

    lax.fori_loop(0, chunks, interleaved, 0)
    lax.fori_loop(n_groups * FAR_UNROLL, n_far, far_single, 0)
    s = s_scr[...]
    for u in range(NEAR_TILES):
        s_next = scores(n_far + u + 1) if u + 1 < NEAR_TILES else None
        absorb(s + _row_tile(bias_scr[:, u * TK:(u + 1) * TK]), n_far + u)
        s = s_next

    acc = acc_scr[...]
    o_all = acc[:, :VDIM] / acc[:, VDIM:]
    o = o_all[:TQ] - lam * o_all[TQ:]
    o_ref[0] = _sub_ln_gate(o, subln_ref, sz_ref[0].astype(F32), lam_init).astype(o_ref.dtype)

    def new_page(ref):
        pad = jnp.zeros((LANES - N_HEADS * rows, VDIM), F32)
        return jnp.concatenate([ref[0], pad], axis=0).astype(BF16)

    sample_update(_dot_nt(sq_all, new_page(kn_ref)) + newb_ref[...], [new_page(vn_ref)])
    so_all = sacc_scr[...] / sl_scr[...]
    for h in range(N_HEADS):
        r0 = 2 * rows * h
        so = so_all[r0:r0 + rows] - lam * so_all[r0 + rows:r0 + 2 * rows]
        so_ref[0, :, head_slice(h)] = _sub_ln_gate(so, subln_ref, ssz_ref[0, :, head_slice(h)],
                                                   lam_init)


def _attention(page_table, qa, qb, kb, vb, bias, sz, sqa, sqb, kn, vn, ssz, hmask, lastb, newb,
               lams, subln, cache_k, cache_v, *, lam_init):
    b, t, _ = qa.shape
    nb, rows, _ = sqa.shape
    n_phys, page_rows, _ = cache_k.shape
    nq = t // TQ
    q_rows = 2 * rows * N_HEADS
    pages_per_chunk = page_table.shape[1] // CHUNKS_PER_SAMPLE
    assert TQ % TK == 0 and t % TQ == 0
    assert b * N_HEADS * nq == nb, "one sample per prompt query tile"
    assert page_table.shape[1] % CHUNKS_PER_SAMPLE == 0

    def flat(bi, h, qi):
        return (bi * N_HEADS + h) * nq + qi

    qspec = pl.BlockSpec((1, TQ, VDIM), lambda bi, h, qi, pt: (bi, qi, h))
    kvspec = pl.BlockSpec((1, t, VDIM), lambda bi, h, qi, pt: (bi, 0, h))
    sspec = pl.BlockSpec((1, rows, D_ATTN), lambda bi, h, qi, pt: (flat(bi, h, qi), 0, 0))
    nspec = pl.BlockSpec((1, rows * N_HEADS, VDIM), lambda bi, h, qi, pt: (flat(bi, h, qi), 0, 0))
    const = lambda *shape: pl.BlockSpec(shape, lambda bi, h, qi, pt: (0,) * len(shape))
    hbm = pl.BlockSpec(memory_space=pl.ANY)
    page_buf = pltpu.VMEM((PAGE_SLOTS, pages_per_chunk, page_rows, VDIM), F32)
    grid_spec = pltpu.PrefetchScalarGridSpec(
        num_scalar_prefetch=1,
        grid=(b, N_HEADS, nq),
        in_specs=[qspec, qspec, kvspec, kvspec,
                  pl.BlockSpec((1, 2, TQ + NEAR_TILES * TK), lambda bi, h, qi, pt: (h, 0, 0)),
                  qspec,
                  sspec, sspec, nspec, nspec, sspec,
                  const(q_rows, page_rows), const(q_rows, page_rows), const(q_rows, LANES),
                  const(1, HALF_DIM), const(1, HALF_DIM), const(1, HALF_DIM), const(1, HALF_DIM),
                  const(1, VDIM), hbm, hbm],
        out_specs=[qspec, sspec],
        scratch_shapes=[pltpu.VMEM((2 * TQ, TK), F32), pltpu.VMEM((2 * TQ, LANES), F32),
                        pltpu.VMEM((2 * TQ, 2 * LANES), F32),
                        pltpu.VMEM((TQ, NEAR_TILES * TK), F32),
                        page_buf, page_buf, pltpu.SemaphoreType.DMA((2, PAGE_SLOTS))]
                       + [pltpu.VMEM((q_rows, LANES), F32)] * 3,
    )
    return pl.pallas_call(
        functools.partial(_attn_body, lam_init=lam_init),
        out_shape=[jax.ShapeDtypeStruct((b, t, D_ATTN), BF16),
                   jax.ShapeDtypeStruct((nb, rows, D_ATTN), F32)],
        grid_spec=grid_spec,
        compiler_params=pltpu.CompilerParams(
            dimension_semantics=("arbitrary", "arbitrary", "arbitrary"),
            vmem_limit_bytes=VMEM_LIMIT),
        name="attention",
    )(page_table.reshape(-1), qa, qb, kb, vb, bias, sz, sqa, sqb, kn, vn, ssz, hmask, lastb, newb,
      *lams, subln, cache_k, cache_v)


def _bias_by_distance(rel_bias, d):
    max_exact = N_BUCKETS // 2
    n = jnp.maximum(d, 0)
    n_f = jnp.maximum(n, 1).astype(F32)
    large = max_exact + (jnp.log(n_f / max_exact) / math.log(MAX_DISTANCE / max_exact)
                         * (N_BUCKETS - max_exact)).astype(jnp.int32)
    bucket = jnp.where(n < max_exact, n, jnp.minimum(large, N_BUCKETS - 1))
    table = rel_bias.astype(F32)
    vals = (table[bucket] - table[N_BUCKETS - 1]).T * LOG2_E
    return jnp.where(d[None, :] >= 0, vals, NEG_INF)


def _toeplitz(rel_bias, delta, n_q, n_k):
    w = n_q + n_k
    row = _bias_by_distance(rel_bias, delta + (n_q - 1) - jnp.arange(w))
    skew = jnp.tile(row, (1, n_q))[:, :n_q * (w - 1)].reshape(N_HEADS, n_q, w - 1)
    return skew[:, :, n_q - 1:n_q - 1 + n_k]


def kernel(x_prompt, x_sample, cache_k, cache_v, state_conv, page_table, c_prompt, c_sample,
           rel_bias, w_ada, b_ada, w_in, w_conv, lambda_q1, lambda_k1, lambda_q2, lambda_k2,
           subln_w, w_out, norm_f):
    depth = w_in.shape[0]
    assert depth == 1
    batch, seq, _ = x_prompt.shape
    nb, dec_seq, _ = x_sample.shape
    n_phys, page = cache_k.shape[1], cache_k.shape[2]
    lam_init = 0.8 - 0.6 * math.exp(-0.3 * 0)

    n_c = batch + nb
    n_c_pad = -(-n_c // MOD_ROWS_PAD) * MOD_ROWS_PAD
    c_all = jnp.concatenate([c_prompt, c_sample, jnp.zeros((n_c_pad - n_c, D_MODEL), F32)], axis=0)
    mod = _modulation(c_all, w_ada[0], b_ada[0].reshape(1, -1))
    mod_p = mod[:batch].reshape(batch, 1, 3 * D_MODEL)
    mod_s = mod[batch:n_c].reshape(nb, 1, 3 * D_MODEL)

    w_in_bf = w_in[0].astype(BF16)
    w_out_bf = w_out[0].astype(BF16)
    lams = [a.reshape(1, HALF_DIM) for a in (lambda_q1[0], lambda_k1[0], lambda_q2[0], lambda_k2[0])]
    subln = subln_w[0].reshape(1, VDIM)

    back = TQ - jnp.arange(TQ + NEAR_TILES * TK)
    bias_prompt = jnp.stack([_bias_by_distance(rel_bias, back),
                             _bias_by_distance(rel_bias, TK + back)], axis=1)

    q_rows = 2 * dec_seq * N_HEADS
    page_rows = page * N_HEADS
    row_head = jnp.arange(q_rows) // (2 * dec_seq)
    same_head = row_head[:, None] == (jnp.arange(page_rows) % N_HEADS)[None, :]
    head_mask = jnp.where(same_head, 0.0, NEG_INF).astype(F32)
    both_maps = lambda t: jnp.concatenate([t, t], axis=1).reshape(q_rows, t.shape[2])
    t_last = both_maps(_toeplitz(rel_bias, page, dec_seq, page))
    last_page = jnp.where(same_head, jnp.repeat(t_last, N_HEADS, axis=1), NEG_INF)
    t_new = both_maps(_toeplitz(rel_bias, 0, dec_seq, dec_seq))
    new_cols = dec_seq * N_HEADS
    bias_new = jnp.where(same_head[:, :new_cols], jnp.repeat(t_new, N_HEADS, axis=1), NEG_INF)
    bias_new = jnp.concatenate([bias_new, jnp.full((q_rows, LANES - new_cols), NEG_INF, F32)],
                               axis=1)

    qa, qb, kf, vf, kb, vb, sz, gp, st_p = _premix(
        x_prompt, mod_p, w_in_bf, w_conv[0], jnp.zeros((batch, CONV_WIDTH - 1, D_CONV), F32),
        g=1, r=ROW_TILE, attn_dtype=BF16, emit_bf16_kv=True)
    g_s = ROW_TILE // dec_seq
    qa_s, qb_s, kf_s, vf_s, sz_s, gs, st_s = _premix(
        x_sample, mod_s, w_in_bf, w_conv[0], state_conv[0],
        g=g_s, r=dec_seq, attn_dtype=F32, emit_bf16_kv=False)
    pages_k = cache_k.reshape(n_phys, page_rows, VDIM)
    pages_v = cache_v.reshape(n_phys, page_rows, VDIM)
    a_p, a_s = _attention(page_table, qa, qb, kb, vb, bias_prompt, sz, qa_s, qb_s, kf_s, vf_s, sz_s,
                          head_mask, last_page, bias_new, lams, subln, pages_k, pages_v,
                          lam_init=lam_init)
    y_prompt = _postmix(a_p, gp, x_prompt, mod_p, w_out_bf, norm_f, g=1, r=ROW_TILE)
    y_sample = _postmix(a_s, gs, x_sample, mod_s, w_out_bf, norm_f, g=g_s, r=dec_seq)

    kv_p = lambda a: a.reshape(1, batch, seq, N_HEADS, VDIM)
    kv_s = lambda a: a.reshape(1, nb, dec_seq, N_HEADS, VDIM)
    return (y_prompt, y_sample, kv_p(kf), kv_p(vf), st_p[None], kv_s(kf_s), kv_s(vf_s), st_s[None])
```

```python
import functools
import math

import jax
import jax.numpy as jnp
from jax import lax
from jax.experimental import pallas as pl
from jax.experimental.pallas import tpu as pltpu

F32 = jnp.float32
BF16 = jnp.bfloat16

D_MODEL = 1024
D_ATTN = 512
D_CONV = 512
HALF_DIM = 64
VDIM = 128
N_HEADS = 4
CONV_WIDTH = 3
N_BUCKETS = 32
MAX_DISTANCE = 128
NORM_EPS = 1e-6
SUBLN_EPS = 1e-5
ATTN_SCALE = HALF_DIM ** -0.5
LOG2_E = math.log2(math.e)
NEG_INF = -1e30
D_IN = 4 * D_ATTN + 4 * D_CONV

LANES = 128
VMEM_LIMIT = 56 * 1024 * 1024

MOD_ROWS_PAD = 8
MOD_TN = 512
ROW_TILE = 512
TQ = 512
TK = 256
NEAR_TILES = TQ // TK + 1
FAR_UNROLL = 4
CHUNKS_PER_SAMPLE = 4
PAGE_SLOTS = 3


def _silu(x):
    return x * jax.nn.sigmoid(x)


def _dot_nt(a, b):
    return lax.dot_general(a, b, (((1,), (1,)), ((), ())), preferred_element_type=F32)


def _lane_tile(x, n):
    return x if n == 1 else jnp.concatenate([x] * n, axis=1)


def _row_tile(x):
    return jnp.concatenate([x, x], axis=0)


def _mod_body(c_ref, w_ref, b_ref, o_ref):
    c = c_ref[...]
    o_ref[...] = jnp.dot(_silu(c).astype(BF16), w_ref[...].astype(BF16),
                         preferred_element_type=F32) + b_ref[...]


def _modulation(c_all, w_ada, b_ada):
    rows = c_all.shape[0]
    n = w_ada.shape[1]
    return pl.pallas_call(
        _mod_body,
        out_shape=jax.ShapeDtypeStruct((rows, n), F32),
        grid=(n // MOD_TN,),
        in_specs=[pl.BlockSpec((rows, D_MODEL), lambda j: (0, 0)),
                  pl.BlockSpec((D_MODEL, MOD_TN), lambda j: (0, j)),
                  pl.BlockSpec((1, MOD_TN), lambda j: (0, j))],
        out_specs=pl.BlockSpec((rows, MOD_TN), lambda j: (0, j)),
        compiler_params=pltpu.CompilerParams(dimension_semantics=("arbitrary",),
                                             vmem_limit_bytes=VMEM_LIMIT),
        name="adaln_mod",
    )(c_all, w_ada, b_ada)


def _premix_body(x_ref, shift_ref, scale_ref, w_ref, wconv_ref, prev_ref, *rest, g, r, attn_dtype,
                 emit_bf16_kv):
    if emit_bf16_kv:
        qa_ref, qb_ref, kf_ref, vf_ref, kb_ref, vb_ref, sz_ref, g_ref, st_ref, carry = rest
    else:
        qa_ref, qb_ref, kf_ref, vf_ref, sz_ref, g_ref, st_ref, carry = rest
    rows = g * r

    @pl.when(pl.program_id(1) == 0)
    def _():
        carry[...] = prev_ref[...]

    x = x_ref[...]
    xn = x * lax.rsqrt(jnp.mean(x * x, axis=-1, keepdims=True) + NORM_EPS)
    h = xn * (1.0 + scale_ref[...]) + shift_ref[...]
    h2 = h.reshape(rows, D_MODEL).astype(BF16)

    def proj(i):
        return jnp.dot(h2, w_ref[:, i * D_ATTN:(i + 1) * D_ATTN], preferred_element_type=F32)

    def put(ref, val):
        ref[...] = val.reshape(g, r, val.shape[-1]).astype(ref.dtype)

    q = proj(0) * (ATTN_SCALE * LOG2_E)
    lane = lax.broadcasted_iota(jnp.int32, (rows, D_ATTN), 1)
    first_half = (lane & (VDIM - 1)) < HALF_DIM
    put(qa_ref, jnp.where(first_half, q, 0.0))
    put(qb_ref, jnp.where(first_half, 0.0, q))
    def put_cache_rows(ref, val):
        val3 = val.reshape(g, r, D_ATTN)
        for hd in range(N_HEADS):
            ref[:, pl.ds(hd, r, stride=N_HEADS), :] = val3[:, :, hd * VDIM:(hd + 1) * VDIM]

    k = proj(1)
    put_cache_rows(kf_ref, k)
    v = proj(2)
    put_cache_rows(vf_ref, v)
    if emit_bf16_kv:
        put(kb_ref, k)
        put(vb_ref, v)
    put(sz_ref, _silu(proj(3)))

    b_gate = proj(4)
    u = proj(5) * proj(6)
    prev = carry[...]
    prev0 = jnp.broadcast_to(prev[:, 0:1, :], (g, r, D_CONV)).reshape(rows, D_CONV)
    prev1 = jnp.broadcast_to(prev[:, 1:2, :], (g, r, D_CONV)).reshape(rows, D_CONV)
    t = lax.broadcasted_iota(jnp.int32, (rows, D_CONV), 0) & (r - 1)
    u1 = jnp.where(t >= 1, pltpu.roll(u, 1, axis=0), prev1)
    u2 = jnp.where(t >= 2, pltpu.roll(u, 2, axis=0), jnp.where(t == 1, prev1, prev0))
    wc = wconv_ref[...]
    conv = wc[0:1, :] * u2 + wc[1:2, :] * u1 + wc[2:3, :] * u
    put(g_ref, b_gate * conv * _silu(proj(7)))

    last2 = u.reshape(g, r, D_CONV)[:, r - 2:, :]
    carry[...] = last2
    st_ref[...] = last2


def _premix(x3, mod3, w_in_bf, w_conv, prev_state, *, g, r, attn_dtype, emit_bf16_kv):
    n_groups, rows_per_group, _ = x3.shape
    grid = (n_groups // g, rows_per_group // r)
    assert r & (r - 1) == 0 and r >= 8
    act = lambda n: pl.BlockSpec((g, r, n), lambda o, i: (o, i, 0))
    mod_spec = lambda col: pl.BlockSpec((g, 1, D_MODEL), lambda o, i: (o, 0, col))
    sds = lambda dt: jax.ShapeDtypeStruct((n_groups, rows_per_group, D_ATTN), dt)
    cache_rows = jax.ShapeDtypeStruct((n_groups, rows_per_group * N_HEADS, VDIM), F32)
    out_shape = [sds(attn_dtype), sds(attn_dtype), cache_rows, cache_rows]
    if emit_bf16_kv:
        out_shape += [sds(BF16), sds(BF16)]
    out_shape += [sds(attn_dtype), sds(attn_dtype),
                  jax.ShapeDtypeStruct((n_groups, CONV_WIDTH - 1, D_CONV), F32)]
    out_specs = [act(D_ATTN)] * (len(out_shape) - 1)
    out_specs[2] = out_specs[3] = pl.BlockSpec((g, r * N_HEADS, VDIM), lambda o, i: (o, i, 0))
    out_specs.append(pl.BlockSpec((g, CONV_WIDTH - 1, D_CONV), lambda o, i: (o, 0, 0)))
    body = functools.partial(_premix_body, g=g, r=r, attn_dtype=attn_dtype,
                             emit_bf16_kv=emit_bf16_kv)
    return pl.pallas_call(
        body,
        out_shape=out_shape,
        grid=grid,
        in_specs=[act(D_MODEL), mod_spec(0), mod_spec(1),
                  pl.BlockSpec((D_MODEL, D_IN), lambda o, i: (0, 0)),
                  pl.BlockSpec((CONV_WIDTH, D_CONV), lambda o, i: (0, 0)),
                  pl.BlockSpec((g, CONV_WIDTH - 1, D_CONV), lambda o, i: (o, 0, 0))],
        out_specs=out_specs,
        scratch_shapes=[pltpu.VMEM((g, CONV_WIDTH - 1, D_CONV), F32)],
        compiler_params=pltpu.CompilerParams(dimension_semantics=("arbitrary", "arbitrary"),
                                             vmem_limit_bytes=VMEM_LIMIT),
        name="premix",
    )(x3, mod3, mod3, w_in_bf, w_conv, prev_state)


def _postmix_body(a_ref, g_ref, x_ref, gate_ref, w_ref, nf_ref, y_ref, *, g, r):
    rows = g * r
    a = a_ref[...].reshape(rows, D_ATTN).astype(BF16)
    gg = g_ref[...].reshape(rows, D_CONV).astype(BF16)
    mix = jnp.concatenate([a, gg], axis=1)
    out = jnp.dot(mix, w_ref[...], preferred_element_type=F32).reshape(g, r, D_MODEL)
    xo = x_ref[...] + gate_ref[...] * out
    y = xo * lax.rsqrt(jnp.mean(xo * xo, axis=-1, keepdims=True) + NORM_EPS)
    y_ref[...] = y * nf_ref[...]


def _postmix(a3, g3, x3, mod3, w_out_bf, norm_f, *, g, r):
    n_groups, rows_per_group, _ = x3.shape
    grid = (n_groups // g, rows_per_group // r)
    act = lambda n: pl.BlockSpec((g, r, n), lambda o, i: (o, i, 0))
    return pl.pallas_call(
        functools.partial(_postmix_body, g=g, r=r),
        out_shape=jax.ShapeDtypeStruct(x3.shape, F32),
        grid=grid,
        in_specs=[act(D_ATTN), act(D_CONV), act(D_MODEL),
                  pl.BlockSpec((g, 1, D_MODEL), lambda o, i: (o, 0, 2)),
                  pl.BlockSpec((D_ATTN + D_CONV, D_MODEL), lambda o, i: (0, 0)),
                  pl.BlockSpec((1, 1, D_MODEL), lambda o, i: (0, 0, 0))],
        out_specs=act(D_MODEL),
        compiler_params=pltpu.CompilerParams(dimension_semantics=("arbitrary", "arbitrary"),
                                             vmem_limit_bytes=VMEM_LIMIT),
        name="postmix",
    )(a3, g3, x3, mod3, w_out_bf, norm_f.reshape(1, 1, D_MODEL))


def _lambda(lq1_ref, lk1_ref, lq2_ref, lk2_ref, lam_init):
    l1 = jnp.sum(lq1_ref[...] * lk1_ref[...], axis=1, keepdims=True)
    l2 = jnp.sum(lq2_ref[...] * lk2_ref[...], axis=1, keepdims=True)
    return jnp.exp(l1) - jnp.exp(l2) + lam_init


def _sub_ln_gate(o, subln_ref, sz, lam_init):
    on = o * lax.rsqrt(jnp.mean(o * o, axis=-1, keepdims=True) + SUBLN_EPS)
    return on * (subln_ref[...] * (1.0 - lam_init)) * sz


def _attn_body(pt_ref,
               qa_ref, qb_ref, k_ref, v_ref, bias_ref, sz_ref,
               sqa_ref, sqb_ref, sqa_next_ref, sqb_next_ref, kn_ref, vn_ref, ssz_ref,
               hmask_ref, lastb_ref, newb_ref,
               lq1_ref, lk1_ref, lq2_ref, lk2_ref, subln_ref, k_hbm, v_hbm,
               o_ref, so_ref,
               s_scr, m_scr, acc_scr, bias_scr, k_buf, v_buf, sems, ss_even, ss_odd,
               sm_scr, sl_scr, sacc_scr,
               *, lam_init):
    qi = pl.program_id(2)
    step = (pl.program_id(0) * pl.num_programs(1) + pl.program_id(1)) * pl.num_programs(2) + qi
    n_steps = pl.num_programs(0) * pl.num_programs(1) * pl.num_programs(2)
    lam = _lambda(lq1_ref, lk1_ref, lq2_ref, lk2_ref, lam_init)

    n_slots, n_pages = k_buf.shape[0], k_buf.shape[1]
    chunks = CHUNKS_PER_SAMPLE
    n_chunks = n_steps * chunks
    rows = sqa_ref.shape[1]

    def page_copies(hbm, buf, sem_row, chunk):
        slot = lax.rem(chunk, n_slots)
        return [pltpu.make_async_copy(hbm.at[pt_ref[chunk * n_pages + i]], buf.at[slot, i],
                                      sems.at[sem_row, slot]) for i in range(n_pages)]

    def k_copies(chunk):
        return page_copies(k_hbm, k_buf, 0, chunk)

    def v_copies(chunk):
        return page_copies(v_hbm, v_buf, 1, chunk)

    def start_chunk(chunk):
        for cp in k_copies(chunk) + v_copies(chunk):
            cp.start()

    def head_slice(h):
        return slice(h * VDIM, (h + 1) * VDIM)

    def stack_queries(a_ref, b_ref):
        return jnp.concatenate([r[0, :, head_slice(h)] for h in range(N_HEADS)
                                for r in (a_ref, b_ref)], axis=0).astype(BF16)

    def page_scores(q_all, chunk):
        slot = lax.rem(chunk, n_slots)
        return jnp.concatenate([_dot_nt(q_all, k_buf[slot, i].astype(BF16))
                                for i in range(n_pages)], axis=1)

    sq_all = stack_queries(sqa_ref, sqb_ref)
    sq_next = stack_queries(sqa_next_ref, sqb_next_ref)

    @pl.when(step == 0)
    def _():
        for ahead in range(n_slots - 1):
            start_chunk(ahead)
        for cp in k_copies(0):
            cp.wait()
        ss_even[...] = page_scores(sq_all, 0)

    sm_scr[...] = jnp.full(sm_scr.shape, NEG_INF, F32)
    sl_scr[...] = jnp.zeros(sl_scr.shape, F32)
    sacc_scr[...] = jnp.zeros(sacc_scr.shape, F32)

    def sample_update(s, weighted_values):
        m_prev = sm_scr[...]
        m_new = jnp.maximum(m_prev, jnp.max(s, axis=1, keepdims=True))
        alpha = jnp.exp2(m_prev - m_new)
        p = jnp.exp2(s - _lane_tile(m_new, s.shape[1] // LANES))
        sl_scr[...] = alpha * sl_scr[...] + jnp.sum(p, axis=1, keepdims=True)
        sacc_scr[...] = alpha * sacc_scr[...] + weighted_values(p.astype(BF16))
        sm_scr[...] = m_new

    def sample_chunk(j, s_ref, s_next_ref):
        chunk = step * chunks + j
        slot = lax.rem(chunk, n_slots)
        ahead = chunk + (n_slots - 1)
        nxt = chunk + 1

        @pl.when(ahead < n_chunks)
        def _():
            start_chunk(ahead)

        @pl.when(nxt < n_chunks)
        def _():
            for cp in k_copies(nxt):
                cp.wait()

        for cp in v_copies(chunk):
            cp.wait()
        q_next = jnp.where(j == chunks - 1, sq_next, sq_all)
        s_next = page_scores(q_next, jnp.minimum(nxt, n_chunks - 1))
        hmask = hmask_ref[...]
        last = jnp.where(j == chunks - 1, lastb_ref[...], hmask)
        s = s_ref[...] + jnp.concatenate([hmask] * (n_pages - 1) + [last], axis=1)
        width = hmask.shape[1]

        def weighted_values(pb):
            pv = None
            for i in range(n_pages):
                d = jnp.dot(pb[:, i * width:(i + 1) * width], v_buf[slot, i].astype(BF16),
                            preferred_element_type=F32)
                pv = d if pv is None else pv + d
            return pv

        sample_update(s, weighted_values)
        s_next_ref[...] = s_next

    q2 = jnp.concatenate([qa_ref[0], qb_ref[0]], axis=0)
    m_scr[...] = jnp.full(m_scr.shape, NEG_INF, F32)
    acc_scr[...] = jnp.zeros(acc_scr.shape, F32)

    def scores(j):
        return _dot_nt(q2, k_ref[0, pl.ds(pl.multiple_of(j * TK, TK), TK), :])

    def absorb(s, j):
        v_t = v_ref[0, pl.ds(pl.multiple_of(j * TK, TK), TK), :]
        m_prev = m_scr[...]
        m_new = jnp.maximum(m_prev, jnp.max(s, axis=1, keepdims=True))
        alpha = jnp.exp2(m_prev - m_new)
        p = jnp.exp2(s - _lane_tile(m_new, TK // LANES)).astype(BF16)
        v_aug = jnp.concatenate([v_t, jnp.ones((TK, LANES), BF16)], axis=1)
        pv = jnp.dot(p, v_aug, preferred_element_type=F32)
        acc_scr[...] = acc_scr[...] * _lane_tile(alpha, 2) + pv
        m_scr[...] = m_new

    n_far = jnp.maximum(qi * (TQ // TK) - 1, 0)
    s_scr[...] = scores(0)

    @pl.when(qi <= 1)
    def _():
        row = jnp.where(qi == 0, bias_ref[0, 0:1, :], bias_ref[0, 1:2, :])
        skewed = pltpu.roll(jnp.broadcast_to(row, (TQ, row.shape[1])), 0, 1,
                            stride=1, stride_axis=0)
        bias_scr[...] = skewed[:, TQ:TQ + NEAR_TILES * TK]

    def far_tiles(j0, count):
        s = s_scr[...]
        for u in range(count):
            s_next = scores(j0 + u + 1)
            absorb(s, j0 + u)
            s = s_next
        s_scr[...] = s

    def far_group(jj, carry):
        far_tiles(jj * FAR_UNROLL, FAR_UNROLL)
        return carry

    def far_single(j, carry):
        far_tiles(j, 1)
        return carry

    n_groups = n_far // FAR_UNROLL

    def interleaved(jj, carry):
        for parity, (s_ref, s_next_ref) in enumerate(((ss_even, ss_odd), (ss_odd, ss_even))):
            j = 2 * jj + parity
            sample_chunk(j, s_ref, s_next_ref)
            lax.fori_loop((n_groups * j) // chunks, (n_groups * (j + 1)) // chunks, far_group, 0)
        return carry

    lax.fori_loop(0, chunks // 2, interleaved, 0)
    lax.fori_loop(n_groups * FAR_UNROLL, n_far, far_single, 0)
    s = s_scr[...]
    for u in range(NEAR_TILES):
        s_next = scores(n_far + u + 1) if u + 1 < NEAR_TILES else None
        absorb(s + _row_tile(bias_scr[:, u * TK:(u + 1) * TK]), n_far + u)
        s = s_next

    acc = acc_scr[...]
    o_all = acc[:, :VDIM] / acc[:, VDIM:]
    o = o_all[:TQ] - lam * o_all[TQ:]
    o_ref[0] = _sub_ln_gate(o, subln_ref, sz_ref[0].astype(F32), lam_init).astype(o_ref.dtype)

    def new_page(ref):
        pad = jnp.zeros((LANES - N_HEADS * rows, VDIM), F32)
        return jnp.concatenate([ref[0], pad], axis=0).astype(BF16)

    sample_update(_dot_nt(sq_all, new_page(kn_ref)) + newb_ref[...],
                  lambda pb: jnp.dot(pb, new_page(vn_ref), preferred_element_type=F32))
    so_all = sacc_scr[...] / sl_scr[...]
    for h in range(N_HEADS):
        r0 = 2 * rows * h
        so = so_all[r0:r0 + rows] - lam * so_all[r0 + rows:r0 + 2 * rows]
        so_ref[0, :, head_slice(h)] = _sub_ln_gate(so, subln_ref, ssz_ref[0, :, head_slice(h)],
                                                   lam_init)


def _attention(page_table, qa, qb, kb, vb, bias, sz, sqa, sqb, kn, vn, ssz, hmask, lastb, newb,
               lams, subln, cache_k, cache_v, *, lam_init):
    b, t, _ = qa.shape
    nb, rows, _ = sqa.shape
    n_phys, page_rows, _ = cache_k.shape
    nq = t // TQ
    q_rows = 2 * rows * N_HEADS
    pages_per_chunk = page_table.shape[1] // CHUNKS_PER_SAMPLE
    assert TQ % TK == 0 and t % TQ == 0
    assert b * N_HEADS * nq == nb, "one sample per prompt query tile"
    assert page_table.shape[1] % CHUNKS_PER_SAMPLE == 0 and CHUNKS_PER_SAMPLE % 2 == 0

    def flat(bi, h, qi):
        return (bi * N_HEADS + h) * nq + qi

    qspec = pl.BlockSpec((1, TQ, VDIM), lambda bi, h, qi, pt: (bi, qi, h))
    kvspec = pl.BlockSpec((1, t, VDIM), lambda bi, h, qi, pt: (bi, 0, h))
    sspec = pl.BlockSpec((1, rows, D_ATTN), lambda bi, h, qi, pt: (flat(bi, h, qi), 0, 0))
    nspec = pl.BlockSpec((1, rows * N_HEADS, VDIM), lambda bi, h, qi, pt: (flat(bi, h, qi), 0, 0))
    const = lambda *shape: pl.BlockSpec(shape, lambda bi, h, qi, pt: (0,) * len(shape))
    hbm = pl.BlockSpec(memory_space=pl.ANY)
    page_buf = pltpu.VMEM((PAGE_SLOTS, pages_per_chunk, page_rows, VDIM), F32)
    score_buf = pltpu.VMEM((q_rows, pages_per_chunk * page_rows), F32)
    sspec_next = pl.BlockSpec(
        (1, rows, D_ATTN), lambda bi, h, qi, pt: (jnp.minimum(flat(bi, h, qi) + 1, nb - 1), 0, 0))
    grid_spec = pltpu.PrefetchScalarGridSpec(
        num_scalar_prefetch=1,
        grid=(b, N_HEADS, nq),
        in_specs=[qspec, qspec, kvspec, kvspec,
                  pl.BlockSpec((1, 2, TQ + NEAR_TILES * TK), lambda bi, h, qi, pt: (h, 0, 0)),
                  qspec,
                  sspec, sspec, sspec_next, sspec_next, nspec, nspec, sspec,
                  const(q_rows, page_rows), const(q_rows, page_rows), const(q_rows, LANES),
                  const(1, HALF_DIM), const(1, HALF_DIM), const(1, HALF_DIM), const(1, HALF_DIM),
                  const(1, VDIM), hbm, hbm],
        out_specs=[qspec, sspec],
        scratch_shapes=[pltpu.VMEM((2 * TQ, TK), F32), pltpu.VMEM((2 * TQ, LANES), F32),
                        pltpu.VMEM((2 * TQ, 2 * LANES), F32),
                        pltpu.VMEM((TQ, NEAR_TILES * TK), F32),
                        page_buf, page_buf, pltpu.SemaphoreType.DMA((2, PAGE_SLOTS)),
                        score_buf, score_buf]
                       + [pltpu.VMEM((q_rows, LANES), F32)] * 3,
    )
    return pl.pallas_call(
        functools.partial(_attn_body, lam_init=lam_init),
        out_shape=[jax.ShapeDtypeStruct((b, t, D_ATTN), BF16),
                   jax.ShapeDtypeStruct((nb, rows, D_ATTN), F32)],
        grid_spec=grid_spec,
        compiler_params=pltpu.CompilerParams(
            dimension_semantics=("arbitrary", "arbitrary", "arbitrary"),
            vmem_limit_bytes=VMEM_LIMIT),
        name="attention",
    )(page_table.reshape(-1), qa, qb, kb, vb, bias, sz, sqa, sqb, sqa, sqb, kn, vn, ssz,
      hmask, lastb, newb,
      *lams, subln, cache_k, cache_v)


def _bias_by_distance(rel_bias, d):
    max_exact = N_BUCKETS // 2
    n = jnp.maximum(d, 0)
    n_f = jnp.maximum(n, 1).astype(F32)
    large = max_exact + (jnp.log(n_f / max_exact) / math.log(MAX_DISTANCE / max_exact)
                         * (N_BUCKETS - max_exact)).astype(jnp.int32)
    bucket = jnp.where(n < max_exact, n, jnp.minimum(large, N_BUCKETS - 1))
    table = rel_bias.astype(F32)
    vals = (table[bucket] - table[N_BUCKETS - 1]).T * LOG2_E
    return jnp.where(d[None, :] >= 0, vals, NEG_INF)


def _toeplitz(rel_bias, delta, n_q, n_k):
    w = n_q + n_k
    row = _bias_by_distance(rel_bias, delta + (n_q - 1) - jnp.arange(w))
    skew = jnp.tile(row, (1, n_q))[:, :n_q * (w - 1)].reshape(N_HEADS, n_q, w - 1)
    return skew[:, :, n_q - 1:n_q - 1 + n_k]


def kernel(x_prompt, x_sample, cache_k, cache_v, state_conv, page_table, c_prompt, c_sample,
           rel_bias, w_ada, b_ada, w_in, w_conv, lambda_q1, lambda_k1, lambda_q2, lambda_k2,
           subln_w, w_out, norm_f):
    depth = w_in.shape[0]
    assert depth == 1
    batch, seq, _ = x_prompt.shape
    nb, dec_seq, _ = x_sample.shape
    n_phys, page = cache_k.shape[1], cache_k.shape[2]
    lam_init = 0.8 - 0.6 * math.exp(-0.3 * 0)

    n_c = batch + nb
    n_c_pad = -(-n_c // MOD_ROWS_PAD) * MOD_ROWS_PAD
    c_all = jnp.concatenate([c_prompt, c_sample, jnp.zeros((n_c_pad - n_c, D_MODEL), F32)], axis=0)
    mod = _modulation(c_all, w_ada[0], b_ada[0].reshape(1, -1))
    mod_p = mod[:batch].reshape(batch, 1, 3 * D_MODEL)
    mod_s = mod[batch:n_c].reshape(nb, 1, 3 * D_MODEL)

    w_in_bf = w_in[0].astype(BF16)
    w_out_bf = w_out[0].astype(BF16)
    lams = [a.reshape(1, HALF_DIM) for a in (lambda_q1[0], lambda_k1[0], lambda_q2[0], lambda_k2[0])]
    subln = subln_w[0].reshape(1, VDIM)

    back = TQ - jnp.arange(TQ + NEAR_TILES * TK)
    bias_prompt = jnp.stack([_bias_by_distance(rel_bias, back),
                             _bias_by_distance(rel_bias, TK + back)], axis=1)

    q_rows = 2 * dec_seq * N_HEADS
    page_rows = page * N_HEADS
    row_head = jnp.arange(q_rows) // (2 * dec_seq)
    same_head = row_head[:, None] == (jnp.arange(page_rows) % N_HEADS)[None, :]
    head_mask = jnp.where(same_head, 0.0, NEG_INF).astype(F32)
    both_maps = lambda t: jnp.concatenate([t, t], axis=1).reshape(q_rows, t.shape[2])
    t_last = both_maps(_toeplitz(rel_bias, page, dec_seq, page))
    last_page = jnp.where(same_head, jnp.repeat(t_last, N_HEADS, axis=1), NEG_INF)
    t_new = both_maps(_toeplitz(rel_bias, 0, dec_seq, dec_seq))
    new_cols = dec_seq * N_HEADS
    bias_new = jnp.where(same_head[:, :new_cols], jnp.repeat(t_new, N_HEADS, axis=1), NEG_INF)
    bias_new = jnp.concatenate([bias_new, jnp.full((q_rows, LANES - new_cols), NEG_INF, F32)],
                               axis=1)

    qa, qb, kf, vf, kb, vb, sz, gp, st_p = _premix(
        x_prompt, mod_p, w_in_bf, w_conv[0], jnp.zeros((batch, CONV_WIDTH - 1, D_CONV), F32),
        g=1, r=ROW_TILE, attn_dtype=BF16, emit_bf16_kv=True)
    g_s = ROW_TILE // dec_seq
    qa_s, qb_s, kf_s, vf_s, sz_s, gs, st_s = _premix(
        x_sample, mod_s, w_in_bf, w_conv[0], state_conv[0],
        g=g_s, r=dec_seq, attn_dtype=F32, emit_bf16_kv=False)
    pages_k = cache_k.reshape(n_phys, page_rows, VDIM)
    pages_v = cache_v.reshape(n_phys, page_rows, VDIM)
    a_p, a_s = _attention(page_table, qa, qb, kb, vb, bias_prompt, sz, qa_s, qb_s, kf_s, vf_s, sz_s,
                          head_mask, last_page, bias_new, lams, subln, pages_k, pages_v,
                          lam_init=lam_init)
    y_prompt = _postmix(a_p, gp, x_prompt, mod_p, w_out_bf, norm_f, g=1, r=ROW_TILE)
    y_sample = _postmix(a_s, gs, x_sample, mod_s, w_out_bf, norm_f, g=g_s, r=dec_seq)

    kv_p = lambda a: a.reshape(1, batch, seq, N_HEADS, VDIM)
    kv_s = lambda a: a.reshape(1, nb, dec_seq, N_HEADS, VDIM)
    return (y_prompt, y_sample, kv_p(kf), kv_p(vf), st_p[None], kv_s(kf_s), kv_s(vf_s), st_s[None])
```

```python
import functools
import math

import jax
import jax.numpy as jnp
from jax import lax
from jax.experimental import pallas as pl
from jax.experimental.pallas import tpu as pltpu

F32 = jnp.float32
BF16 = jnp.bfloat16

D_MODEL = 1024
D_ATTN = 512
D_CONV = 512
HALF_DIM = 64
VDIM = 128
N_HEADS = 4
CONV_WIDTH = 3
N_BUCKETS = 32
MAX_DISTANCE = 128
NORM_EPS = 1e-6
SUBLN_EPS = 1e-5
ATTN_SCALE = HALF_DIM ** -0.5
LOG2_E = math.log2(math.e)
NEG_INF = -1e30
D_IN = 4 * D_ATTN + 4 * D_CONV

LANES = 128
VMEM_LIMIT = 56 * 1024 * 1024

MOD_ROWS_PAD = 8
MOD_TN = 512
ROW_TILE = 512
TQ = 512
TK = 256
NEAR_TILES = TQ // TK + 1
FAR_UNROLL = 4
CHUNKS_PER_SAMPLE = 8
PAGE_SLOTS = 6


def _silu(x):
    return x * jax.nn.sigmoid(x)


def _dot_nt(a, b):
    return lax.dot_general(a, b, (((1,), (1,)), ((), ())), preferred_element_type=F32)


def _lane_tile(x, n):
    return x if n == 1 else jnp.concatenate([x] * n, axis=1)


def _row_tile(x):
    return jnp.concatenate([x, x], axis=0)


def _mod_body(c_ref, w_ref, b_ref, o_ref):
    c = c_ref[...]
    o_ref[...] = jnp.dot(_silu(c).astype(BF16), w_ref[...].astype(BF16),
                         preferred_element_type=F32) + b_ref[...]


def _modulation(c_all, w_ada, b_ada):
    rows = c_all.shape[0]
    n = w_ada.shape[1]
    return pl.pallas_call(
        _mod_body,
        out_shape=jax.ShapeDtypeStruct((rows, n), F32),
        grid=(n // MOD_TN,),
        in_specs=[pl.BlockSpec((rows, D_MODEL), lambda j: (0, 0)),
                  pl.BlockSpec((D_MODEL, MOD_TN), lambda j: (0, j)),
                  pl.BlockSpec((1, MOD_TN), lambda j: (0, j))],
        out_specs=pl.BlockSpec((rows, MOD_TN), lambda j: (0, j)),
        compiler_params=pltpu.CompilerParams(dimension_semantics=("arbitrary",),
                                             vmem_limit_bytes=VMEM_LIMIT),
        name="adaln_mod",
    )(c_all, w_ada, b_ada)


def _premix_body(x_ref, shift_ref, scale_ref, w_ref, wconv_ref, prev_ref, *rest, g, r, attn_dtype,
                 emit_bf16_kv):
    if emit_bf16_kv:
        qa_ref, qb_ref, kf_ref, vf_ref, kb_ref, vb_ref, sz_ref, g_ref, st_ref, carry = rest
    else:
        qa_ref, qb_ref, kf_ref, vf_ref, sz_ref, g_ref, st_ref, carry = rest
    rows = g * r

    @pl.when(pl.program_id(1) == 0)
    def _():
        carry[...] = prev_ref[...]

    x = x_ref[...]
    xn = x * lax.rsqrt(jnp.mean(x * x, axis=-1, keepdims=True) + NORM_EPS)
    h = xn * (1.0 + scale_ref[...]) + shift_ref[...]
    h2 = h.reshape(rows, D_MODEL).astype(BF16)

    def proj(i):
        return jnp.dot(h2, w_ref[:, i * D_ATTN:(i + 1) * D_ATTN], preferred_element_type=F32)

    def put(ref, val):
        ref[...] = val.reshape(g, r, val.shape[-1]).astype(ref.dtype)

    q = proj(0) * (ATTN_SCALE * LOG2_E)
    lane = lax.broadcasted_iota(jnp.int32, (rows, D_ATTN), 1)
    first_half = (lane & (VDIM - 1)) < HALF_DIM
    put(qa_ref, jnp.where(first_half, q, 0.0))
    put(qb_ref, jnp.where(first_half, 0.0, q))
    def put_cache_rows(ref, val):
        val3 = val.reshape(g, r, D_ATTN)
        for hd in range(N_HEADS):
            ref[:, pl.ds(hd, r, stride=N_HEADS), :] = val3[:, :, hd * VDIM:(hd + 1) * VDIM]

    k = proj(1)
    put_cache_rows(kf_ref, k)
    v = proj(2)
    put_cache_rows(vf_ref, v)
    if emit_bf16_kv:
        put(kb_ref, k)
        put(vb_ref, v)
    put(sz_ref, _silu(proj(3)))

    b_gate = proj(4)
    u = proj(5) * proj(6)
    prev = carry[...]
    prev0 = jnp.broadcast_to(prev[:, 0:1, :], (g, r, D_CONV)).reshape(rows, D_CONV)
    prev1 = jnp.broadcast_to(prev[:, 1:2, :], (g, r, D_CONV)).reshape(rows, D_CONV)
    t = lax.broadcasted_iota(jnp.int32, (rows, D_CONV), 0) & (r - 1)
    u1 = jnp.where(t >= 1, pltpu.roll(u, 1, axis=0), prev1)
    u2 = jnp.where(t >= 2, pltpu.roll(u, 2, axis=0), jnp.where(t == 1, prev1, prev0))
    wc = wconv_ref[...]
    conv = wc[0:1, :] * u2 + wc[1:2, :] * u1 + wc[2:3, :] * u
    put(g_ref, b_gate * conv * _silu(proj(7)))

    last2 = u.reshape(g, r, D_CONV)[:, r - 2:, :]
    carry[...] = last2
    st_ref[...] = last2


def _premix(x3, mod3, w_in_bf, w_conv, prev_state, *, g, r, attn_dtype, emit_bf16_kv):
    n_groups, rows_per_group, _ = x3.shape
    grid = (n_groups // g, rows_per_group // r)
    assert r & (r - 1) == 0 and r >= 8
    act = lambda n: pl.BlockSpec((g, r, n), lambda o, i: (o, i, 0))
    mod_spec = lambda col: pl.BlockSpec((g, 1, D_MODEL), lambda o, i: (o, 0, col))
    sds = lambda dt: jax.ShapeDtypeStruct((n_groups, rows_per_group, D_ATTN), dt)
    cache_rows = jax.ShapeDtypeStruct((n_groups, rows_per_group * N_HEADS, VDIM), F32)
    out_shape = [sds(attn_dtype), sds(attn_dtype), cache_rows, cache_rows]
    if emit_bf16_kv:
        out_shape += [sds(BF16), sds(BF16)]
    out_shape += [sds(attn_dtype), sds(attn_dtype),
                  jax.ShapeDtypeStruct((n_groups, CONV_WIDTH - 1, D_CONV), F32)]
    out_specs = [act(D_ATTN)] * (len(out_shape) - 1)
    out_specs[2] = out_specs[3] = pl.BlockSpec((g, r * N_HEADS, VDIM), lambda o, i: (o, i, 0))
    out_specs.append(pl.BlockSpec((g, CONV_WIDTH - 1, D_CONV), lambda o, i: (o, 0, 0)))
    body = functools.partial(_premix_body, g=g, r=r, attn_dtype=attn_dtype,
                             emit_bf16_kv=emit_bf16_kv)
    return pl.pallas_call(
        body,
        out_shape=out_shape,
        grid=grid,
        in_specs=[act(D_MODEL), mod_spec(0), mod_spec(1),
                  pl.BlockSpec((D_MODEL, D_IN), lambda o, i: (0, 0)),
                  pl.BlockSpec((CONV_WIDTH, D_CONV), lambda o, i: (0, 0)),
                  pl.BlockSpec((g, CONV_WIDTH - 1, D_CONV), lambda o, i: (o, 0, 0))],
        out_specs=out_specs,
        scratch_shapes=[pltpu.VMEM((g, CONV_WIDTH - 1, D_CONV), F32)],
        compiler_params=pltpu.CompilerParams(dimension_semantics=("arbitrary", "arbitrary"),
                                             vmem_limit_bytes=VMEM_LIMIT),
        name="premix",
    )(x3, mod3, mod3, w_in_bf, w_conv, prev_state)


def _postmix_body(a_ref, g_ref, x_ref, gate_ref, w_ref, nf_ref, y_ref, *, g, r):
    rows = g * r
    a = a_ref[...].reshape(rows, D_ATTN).astype(BF16)
    gg = g_ref[...].reshape(rows, D_CONV).astype(BF16)
    mix = jnp.concatenate([a, gg], axis=1)
    out = jnp.dot(mix, w_ref[...], preferred_element_type=F32).reshape(g, r, D_MODEL)
    xo = x_ref[...] + gate_ref[...] * out
    y = xo * lax.rsqrt(jnp.mean(xo * xo, axis=-1, keepdims=True) + NORM_EPS)
    y_ref[...] = y * nf_ref[...]


def _postmix(a3, g3, x3, mod3, w_out_bf, norm_f, *, g, r):
    n_groups, rows_per_group, _ = x3.shape
    grid = (n_groups // g, rows_per_group // r)
    act = lambda n: pl.BlockSpec((g, r, n), lambda o, i: (o, i, 0))
    return pl.pallas_call(
        functools.partial(_postmix_body, g=g, r=r),
        out_shape=jax.ShapeDtypeStruct(x3.shape, F32),
        grid=grid,
        in_specs=[act(D_ATTN), act(D_CONV), act(D_MODEL),
                  pl.BlockSpec((g, 1, D_MODEL), lambda o, i: (o, 0, 2)),
                  pl.BlockSpec((D_ATTN + D_CONV, D_MODEL), lambda o, i: (0, 0)),
                  pl.BlockSpec((1, 1, D_MODEL), lambda o, i: (0, 0, 0))],
        out_specs=act(D_MODEL),
        compiler_params=pltpu.CompilerParams(dimension_semantics=("arbitrary", "arbitrary"),
                                             vmem_limit_bytes=VMEM_LIMIT),
        name="postmix",
    )(a3, g3, x3, mod3, w_out_bf, norm_f.reshape(1, 1, D_MODEL))


def _lambda(lq1_ref, lk1_ref, lq2_ref, lk2_ref, lam_init):
    l1 = jnp.sum(lq1_ref[...] * lk1_ref[...], axis=1, keepdims=True)
    l2 = jnp.sum(lq2_ref[...] * lk2_ref[...], axis=1, keepdims=True)
    return jnp.exp(l1) - jnp.exp(l2) + lam_init


def _tile_order(i, n):
    return jnp.where(i % 2 == 0, i // 2, n - 1 - i // 2)


def _sub_ln_gate(o, subln_ref, sz, lam_init):
    on = o * lax.rsqrt(jnp.mean(o * o, axis=-1, keepdims=True) + SUBLN_EPS)
    return on * (subln_ref[...] * (1.0 - lam_init)) * sz


def _attn_body(pt_ref,
               qa_ref, qb_ref, k_ref, v_ref, bias_ref, sz_ref,
               sqa_ref, sqb_ref, sqa_next_ref, sqb_next_ref, kn_ref, vn_ref, ssz_ref,
               hmask_ref, lastb_ref, newb_ref,
               lq1_ref, lk1_ref, lq2_ref, lk2_ref, subln_ref, k_hbm, v_hbm,
               o_ref, so_ref,
               s_scr, m_scr, acc_scr, bias_scr, k_buf, v_buf, sems, ss_even, ss_odd,
               sm_scr, sl_scr, sacc_scr,
               *, lam_init):
    order = pl.program_id(2)
    qi = _tile_order(order, pl.num_programs(2))
    step = (pl.program_id(0) * pl.num_programs(1) + pl.program_id(1)) * pl.num_programs(2) + order
    n_steps = pl.num_programs(0) * pl.num_programs(1) * pl.num_programs(2)
    lam = _lambda(lq1_ref, lk1_ref, lq2_ref, lk2_ref, lam_init)

    n_slots, n_pages = k_buf.shape[0], k_buf.shape[1]
    chunks = CHUNKS_PER_SAMPLE
    n_chunks = n_steps * chunks
    rows = sqa_ref.shape[1]

    def page_copies(hbm, buf, sem_row, chunk):
        slot = lax.rem(chunk, n_slots)
        return [pltpu.make_async_copy(hbm.at[pt_ref[chunk * n_pages + i]], buf.at[slot, i],
                                      sems.at[sem_row, slot]) for i in range(n_pages)]

    def k_copies(chunk):
        return page_copies(k_hbm, k_buf, 0, chunk)

    def v_copies(chunk):
        return page_copies(v_hbm, v_buf, 1, chunk)

    def start_chunk(chunk):
        for cp in k_copies(chunk) + v_copies(chunk):
            cp.start()

    def head_slice(h):
        return slice(h * VDIM, (h + 1) * VDIM)

    def stack_queries(a_ref, b_ref):
        return jnp.concatenate([r[0, :, head_slice(h)] for h in range(N_HEADS)
                                for r in (a_ref, b_ref)], axis=0).astype(BF16)

    def page_scores(q_all, chunk):
        slot = lax.rem(chunk, n_slots)
        return jnp.concatenate([_dot_nt(q_all, k_buf[slot, i].astype(BF16))
                                for i in range(n_pages)], axis=1)

    sq_all = stack_queries(sqa_ref, sqb_ref)
    sq_next = stack_queries(sqa_next_ref, sqb_next_ref)

    @pl.when(step == 0)
    def _():
        for ahead in range(n_slots - 1):
            start_chunk(ahead)
        for cp in k_copies(0):
            cp.wait()
        ss_even[...] = page_scores(sq_all, 0)

    sm_scr[...] = jnp.full(sm_scr.shape, NEG_INF, F32)
    sl_scr[...] = jnp.zeros(sl_scr.shape, F32)
    sacc_scr[...] = jnp.zeros(sacc_scr.shape, F32)

    def sample_update(s, weighted_values):
        m_prev = sm_scr[...]
        m_new = jnp.maximum(m_prev, jnp.max(s, axis=1, keepdims=True))
        alpha = jnp.exp2(m_prev - m_new)
        p = jnp.exp2(s - _lane_tile(m_new, s.shape[1] // LANES))
        sl_scr[...] = alpha * sl_scr[...] + jnp.sum(p, axis=1, keepdims=True)
        sacc_scr[...] = alpha * sacc_scr[...] + weighted_values(p.astype(BF16))
        sm_scr[...] = m_new

    def sample_chunk(j, s_ref, s_next_ref):
        chunk = step * chunks + j
        slot = lax.rem(chunk, n_slots)
        ahead = chunk + (n_slots - 1)
        nxt = chunk + 1

        @pl.when(ahead < n_chunks)
        def _():
            start_chunk(ahead)

        @pl.when(nxt < n_chunks)
        def _():
            for cp in k_copies(nxt):
                cp.wait()

        for cp in v_copies(chunk):
            cp.wait()
        q_next = jnp.where(j == chunks - 1, sq_next, sq_all)
        s_next = page_scores(q_next, jnp.minimum(nxt, n_chunks - 1))
        hmask = hmask_ref[...]
        last = jnp.where(j == chunks - 1, lastb_ref[...], hmask)
        s = s_ref[...] + jnp.concatenate([hmask] * (n_pages - 1) + [last], axis=1)
        width = hmask.shape[1]

        def weighted_values(pb):
            pv = None
            for i in range(n_pages):
                d = jnp.dot(pb[:, i * width:(i + 1) * width], v_buf[slot, i].astype(BF16),
                            preferred_element_type=F32)
                pv = d if pv is None else pv + d
            return pv

        sample_update(s, weighted_values)
        s_next_ref[...] = s_next

    q2 = jnp.concatenate([qa_ref[0], qb_ref[0]], axis=0)
    m_scr[...] = jnp.full(m_scr.shape, NEG_INF, F32)
    acc_scr[...] = jnp.zeros(acc_scr.shape, F32)

    def scores(j):
        return _dot_nt(q2, k_ref[0, pl.ds(pl.multiple_of(j * TK, TK), TK), :])

    def absorb(s, j):
        v_t = v_ref[0, pl.ds(pl.multiple_of(j * TK, TK), TK), :]
        m_prev = m_scr[...]
        m_new = jnp.maximum(m_prev, jnp.max(s, axis=1, keepdims=True))
        alpha = jnp.exp2(m_prev - m_new)
        p = jnp.exp2(s - _lane_tile(m_new, TK // LANES)).astype(BF16)
        v_aug = jnp.concatenate([v_t, jnp.ones((TK, LANES), BF16)], axis=1)
        pv = jnp.dot(p, v_aug, preferred_element_type=F32)
        acc_scr[...] = acc_scr[...] * _lane_tile(alpha, 2) + pv
        m_scr[...] = m_new

    n_far = jnp.maximum(qi * (TQ // TK) - 1, 0)
    s_scr[...] = scores(0)

    @pl.when(order <= 1)
    def _():
        row = jnp.where(order == 0, bias_ref[0, 0:1, :], bias_ref[0, 1:2, :])
        skewed = pltpu.roll(jnp.broadcast_to(row, (TQ, row.shape[1])), 0, 1,
                            stride=1, stride_axis=0)
        bias_scr[...] = skewed[:, TQ:TQ + NEAR_TILES * TK]

    def far_tiles(j0, count):
        s = s_scr[...]
        for u in range(count):
            s_next = scores(j0 + u + 1)
            absorb(s, j0 + u)
            s = s_next
        s_scr[...] = s

    def far_group(jj, carry):
        far_tiles(jj * FAR_UNROLL, FAR_UNROLL)
        return carry

    def far_single(j, carry):
        far_tiles(j, 1)
        return carry

    n_groups = n_far // FAR_UNROLL

    def interleaved(jj, carry):
        for parity, (s_ref, s_next_ref) in enumerate(((ss_even, ss_odd), (ss_odd, ss_even))):
            j = 2 * jj + parity
            sample_chunk(j, s_ref, s_next_ref)
            lax.fori_loop((n_groups * j) // chunks, (n_groups * (j + 1)) // chunks, far_group, 0)
        return carry

    lax.fori_loop(0, chunks // 2, interleaved, 0)
    lax.fori_loop(n_groups * FAR_UNROLL, n_far, far_single, 0)
    s = s_scr[...]
    for u in range(NEAR_TILES):
        s_next = scores(n_far + u + 1) if u + 1 < NEAR_TILES else None
        absorb(s + _row_tile(bias_scr[:, u * TK:(u + 1) * TK]), n_far + u)
        s = s_next

    acc = acc_scr[...]
    o_all = acc[:, :VDIM] / acc[:, VDIM:]
    o = o_all[:TQ] - lam * o_all[TQ:]
    o_ref[0] = _sub_ln_gate(o, subln_ref, sz_ref[0].astype(F32), lam_init).astype(o_ref.dtype)

    def new_page(ref):
        pad = jnp.zeros((LANES - N_HEADS * rows, VDIM), F32)
        return jnp.concatenate([ref[0], pad], axis=0).astype(BF16)

    sample_update(_dot_nt(sq_all, new_page(kn_ref)) + newb_ref[...],
                  lambda pb: jnp.dot(pb, new_page(vn_ref), preferred_element_type=F32))
    so_all = sacc_scr[...] / sl_scr[...]
    for h in range(N_HEADS):
        r0 = 2 * rows * h
        so = so_all[r0:r0 + rows] - lam * so_all[r0 + rows:r0 + 2 * rows]
        so_ref[0, :, head_slice(h)] = _sub_ln_gate(so, subln_ref, ssz_ref[0, :, head_slice(h)],
                                                   lam_init)


def _attention(page_table, qa, qb, kb, vb, bias, sz, sqa, sqb, kn, vn, ssz, hmask, lastb, newb,
               lams, subln, cache_k, cache_v, *, lam_init):
    b, t, _ = qa.shape
    nb, rows, _ = sqa.shape
    n_phys, page_rows, _ = cache_k.shape
    nq = t // TQ
    q_rows = 2 * rows * N_HEADS
    pages_per_chunk = page_table.shape[1] // CHUNKS_PER_SAMPLE
    assert TQ % TK == 0 and t % TQ == 0
    assert b * N_HEADS * nq == nb, "one sample per prompt query tile"
    assert page_table.shape[1] % CHUNKS_PER_SAMPLE == 0 and CHUNKS_PER_SAMPLE % 2 == 0

    def flat(bi, h, qi):
        return (bi * N_HEADS + h) * nq + qi

    qspec = pl.BlockSpec((1, TQ, VDIM), lambda bi, h, qi, pt: (bi, _tile_order(qi, nq), h))
    kvspec = pl.BlockSpec((1, t, VDIM), lambda bi, h, qi, pt: (bi, 0, h))
    sspec = pl.BlockSpec((1, rows, D_ATTN), lambda bi, h, qi, pt: (flat(bi, h, qi), 0, 0))
    nspec = pl.BlockSpec((1, rows * N_HEADS, VDIM), lambda bi, h, qi, pt: (flat(bi, h, qi), 0, 0))
    const = lambda *shape: pl.BlockSpec(shape, lambda bi, h, qi, pt: (0,) * len(shape))
    hbm = pl.BlockSpec(memory_space=pl.ANY)
    page_buf = pltpu.VMEM((PAGE_SLOTS, pages_per_chunk, page_rows, VDIM), F32)
    score_buf = pltpu.VMEM((q_rows, pages_per_chunk * page_rows), F32)
    sspec_next = pl.BlockSpec(
        (1, rows, D_ATTN), lambda bi, h, qi, pt: (jnp.minimum(flat(bi, h, qi) + 1, nb - 1), 0, 0))
    grid_spec = pltpu.PrefetchScalarGridSpec(
        num_scalar_prefetch=1,
        grid=(b, N_HEADS, nq),
        in_specs=[qspec, qspec, kvspec, kvspec,
                  pl.BlockSpec((1, 2, TQ + NEAR_TILES * TK), lambda bi, h, qi, pt: (h, 0, 0)),
                  qspec,
                  sspec, sspec, sspec_next, sspec_next, nspec, nspec, sspec,
                  const(q_rows, page_rows), const(q_rows, page_rows), const(q_rows, LANES),
                  const(1, HALF_DIM), const(1, HALF_DIM), const(1, HALF_DIM), const(1, HALF_DIM),
                  const(1, VDIM), hbm, hbm],
        out_specs=[qspec, sspec],
        scratch_shapes=[pltpu.VMEM((2 * TQ, TK), F32), pltpu.VMEM((2 * TQ, LANES), F32),
                        pltpu.VMEM((2 * TQ, 2 * LANES), F32),
                        pltpu.VMEM((TQ, NEAR_TILES * TK), F32),
                        page_buf, page_buf, pltpu.SemaphoreType.DMA((2, PAGE_SLOTS)),
                        score_buf, score_buf]
                       + [pltpu.VMEM((q_rows, LANES), F32)] * 3,
    )
    return pl.pallas_call(
        functools.partial(_attn_body, lam_init=lam_init),
        out_shape=[jax.ShapeDtypeStruct((b, t, D_ATTN), BF16),
                   jax.ShapeDtypeStruct((nb, rows, D_ATTN), F32)],
        grid_spec=grid_spec,
        compiler_params=pltpu.CompilerParams(
            dimension_semantics=("arbitrary", "arbitrary", "arbitrary"),
            vmem_limit_bytes=VMEM_LIMIT),
        name="attention",
    )(page_table.reshape(-1), qa, qb, kb, vb, bias, sz, sqa, sqb, sqa, sqb, kn, vn, ssz,
      hmask, lastb, newb,
      *lams, subln, cache_k, cache_v)


def _bias_by_distance(rel_bias, d):
    max_exact = N_BUCKETS // 2
    n = jnp.maximum(d, 0)
    n_f = jnp.maximum(n, 1).astype(F32)
    large = max_exact + (jnp.log(n_f / max_exact) / math.log(MAX_DISTANCE / max_exact)
                         * (N_BUCKETS - max_exact)).astype(jnp.int32)
    bucket = jnp.where(n < max_exact, n, jnp.minimum(large, N_BUCKETS - 1))
    table = rel_bias.astype(F32)
    vals = (table[bucket] - table[N_BUCKETS - 1]).T * LOG2_E
    return jnp.where(d[None, :] >= 0, vals, NEG_INF)


def _toeplitz(rel_bias, delta, n_q, n_k):
    w = n_q + n_k
    row = _bias_by_distance(rel_bias, delta + (n_q - 1) - jnp.arange(w))
    skew = jnp.tile(row, (1, n_q))[:, :n_q * (w - 1)].reshape(N_HEADS, n_q, w - 1)
    return skew[:, :, n_q - 1:n_q - 1 + n_k]


def kernel(x_prompt, x_sample, cache_k, cache_v, state_conv, page_table, c_prompt, c_sample,
           rel_bias, w_ada, b_ada, w_in, w_conv, lambda_q1, lambda_k1, lambda_q2, lambda_k2,
           subln_w, w_out, norm_f):
    depth = w_in.shape[0]
    assert depth == 1
    batch, seq, _ = x_prompt.shape
    nb, dec_seq, _ = x_sample.shape
    n_phys, page = cache_k.shape[1], cache_k.shape[2]
    lam_init = 0.8 - 0.6 * math.exp(-0.3 * 0)

    n_c = batch + nb
    n_c_pad = -(-n_c // MOD_ROWS_PAD) * MOD_ROWS_PAD
    c_all = jnp.concatenate([c_prompt, c_sample, jnp.zeros((n_c_pad - n_c, D_MODEL), F32)], axis=0)
    mod = _modulation(c_all, w_ada[0], b_ada[0].reshape(1, -1))
    mod_p = mod[:batch].reshape(batch, 1, 3 * D_MODEL)
    mod_s = mod[batch:n_c].reshape(nb, 1, 3 * D_MODEL)

    w_in_bf = w_in[0].astype(BF16)
    w_out_bf = w_out[0].astype(BF16)
    lams = [a.reshape(1, HALF_DIM) for a in (lambda_q1[0], lambda_k1[0], lambda_q2[0], lambda_k2[0])]
    subln = subln_w[0].reshape(1, VDIM)

    back = TQ - jnp.arange(TQ + NEAR_TILES * TK)
    bias_prompt = jnp.stack([_bias_by_distance(rel_bias, back),
                             _bias_by_distance(rel_bias, TK + back)], axis=1)

    q_rows = 2 * dec_seq * N_HEADS
    page_rows = page * N_HEADS
    row_head = jnp.arange(q_rows) // (2 * dec_seq)
    same_head = row_head[:, None] == (jnp.arange(page_rows) % N_HEADS)[None, :]
    head_mask = jnp.where(same_head, 0.0, NEG_INF).astype(F32)
    both_maps = lambda t: jnp.concatenate([t, t], axis=1).reshape(q_rows, t.shape[2])
    t_last = both_maps(_toeplitz(rel_bias, page, dec_seq, page))
    last_page = jnp.where(same_head, jnp.repeat(t_last, N_HEADS, axis=1), NEG_INF)
    t_new = both_maps(_toeplitz(rel_bias, 0, dec_seq, dec_seq))
    new_cols = dec_seq * N_HEADS
    bias_new = jnp.where(same_head[:, :new_cols], jnp.repeat(t_new, N_HEADS, axis=1), NEG_INF)
    bias_new = jnp.concatenate([bias_new, jnp.full((q_rows, LANES - new_cols), NEG_INF, F32)],
                               axis=1)

    qa, qb, kf, vf, kb, vb, sz, gp, st_p = _premix(
        x_prompt, mod_p, w_in_bf, w_conv[0], jnp.zeros((batch, CONV_WIDTH - 1, D_CONV), F32),
        g=1, r=ROW_TILE, attn_dtype=BF16, emit_bf16_kv=True)
    g_s = ROW_TILE // dec_seq
    qa_s, qb_s, kf_s, vf_s, sz_s, gs, st_s = _premix(
        x_sample, mod_s, w_in_bf, w_conv[0], state_conv[0],
        g=g_s, r=dec_seq, attn_dtype=F32, emit_bf16_kv=False)
    pages_k = cache_k.reshape(n_phys, page_rows, VDIM)
    pages_v = cache_v.reshape(n_phys, page_rows, VDIM)
    a_p, a_s = _attention(page_table, qa, qb, kb, vb, bias_prompt, sz, qa_s, qb_s, kf_s, vf_s, sz_s,
                          head_mask, last_page, bias_new, lams, subln, pages_k, pages_v,
                          lam_init=lam_init)
    y_prompt = _postmix(a_p, gp, x_prompt, mod_p, w_out_bf, norm_f, g=1, r=ROW_TILE)
    y_sample = _postmix(a_s, gs, x_sample, mod_s, w_out_bf, norm_f, g=g_s, r=dec_seq)

    kv_p = lambda a: a.reshape(1, batch, seq, N_HEADS, VDIM)
    kv_s = lambda a: a.reshape(1, nb, dec_seq, N_HEADS, VDIM)
    return (y_prompt, y_sample, kv_p(kf), kv_p(vf), st_p[None], kv_s(kf_s), kv_s(vf_s), st_s[None])
```

```python
import functools
import math

import jax
import jax.numpy as jnp
from jax import lax
from jax.experimental import pallas as pl
from jax.experimental.pallas import tpu as pltpu

F32 = jnp.float32
BF16 = jnp.bfloat16

D_MODEL = 1024
D_ATTN = 512
D_CONV = 512
HALF_DIM = 64
VDIM = 128
N_HEADS = 4
CONV_WIDTH = 3
N_BUCKETS = 32
MAX_DISTANCE = 128
NORM_EPS = 1e-6
SUBLN_EPS = 1e-5
ATTN_SCALE = HALF_DIM ** -0.5
LOG2_E = math.log2(math.e)
NEG_INF = -1e30
D_IN = 4 * D_ATTN + 4 * D_CONV

LANES = 128
VMEM_LIMIT = 56 * 1024 * 1024

MOD_ROWS_PAD = 8
MOD_TN = 512
ROW_TILE = 512
TQ = 512
TK = 256
NEAR_TILES = TQ // TK + 1
FAR_UNROLL = 4
CHUNKS_PER_SAMPLE = 8
PAGE_SLOTS = 8


def _silu(x):
    return x * jax.nn.sigmoid(x)


def _dot_nt(a, b):
    return lax.dot_general(a, b, (((1,), (1,)), ((), ())), preferred_element_type=F32)


def _lane_tile(x, n):
    return x if n == 1 else jnp.concatenate([x] * n, axis=1)


def _row_tile(x):
    return jnp.concatenate([x, x], axis=0)


def _mod_body(c_ref, w_ref, b_ref, o_ref):
    c = c_ref[...]
    o_ref[...] = jnp.dot(_silu(c).astype(BF16), w_ref[...].astype(BF16),
                         preferred_element_type=F32) + b_ref[...]


def _modulation(c_all, w_ada, b_ada):
    rows = c_all.shape[0]
    n = w_ada.shape[1]
    return pl.pallas_call(
        _mod_body,
        out_shape=jax.ShapeDtypeStruct((rows, n), F32),
        grid=(n // MOD_TN,),
        in_specs=[pl.BlockSpec((rows, D_MODEL), lambda j: (0, 0)),
                  pl.BlockSpec((D_MODEL, MOD_TN), lambda j: (0, j)),
                  pl.BlockSpec((1, MOD_TN), lambda j: (0, j))],
        out_specs=pl.BlockSpec((rows, MOD_TN), lambda j: (0, j)),
        compiler_params=pltpu.CompilerParams(dimension_semantics=("arbitrary",),
                                             vmem_limit_bytes=VMEM_LIMIT),
        name="adaln_mod",
    )(c_all, w_ada, b_ada)


def _premix_body(x_ref, shift_ref, scale_ref, w_ref, wconv_ref, prev_ref, *rest, g, r, attn_dtype,
                 emit_bf16_kv):
    if emit_bf16_kv:
        qa_ref, qb_ref, kf_ref, vf_ref, kb_ref, vb_ref, sz_ref, g_ref, st_ref, carry = rest
    else:
        qa_ref, qb_ref, kf_ref, vf_ref, sz_ref, g_ref, st_ref, carry = rest
    rows = g * r

    @pl.when(pl.program_id(1) == 0)
    def _():
        carry[...] = prev_ref[...]

    x = x_ref[...]
    xn = x * lax.rsqrt(jnp.mean(x * x, axis=-1, keepdims=True) + NORM_EPS)
    h = xn * (1.0 + scale_ref[...]) + shift_ref[...]
    h2 = h.reshape(rows, D_MODEL).astype(BF16)

    def proj(i):
        return jnp.dot(h2, w_ref[:, i * D_ATTN:(i + 1) * D_ATTN], preferred_element_type=F32)

    def put(ref, val):
        ref[...] = val.reshape(g, r, val.shape[-1]).astype(ref.dtype)

    q = proj(0) * (ATTN_SCALE * LOG2_E)
    lane = lax.broadcasted_iota(jnp.int32, (rows, D_ATTN), 1)
    first_half = (lane & (VDIM - 1)) < HALF_DIM
    put(qa_ref, jnp.where(first_half, q, 0.0))
    put(qb_ref, jnp.where(first_half, 0.0, q))
    def put_cache_rows(ref, val):
        val3 = val.reshape(g, r, D_ATTN)
        for hd in range(N_HEADS):
            ref[:, pl.ds(hd, r, stride=N_HEADS), :] = val3[:, :, hd * VDIM:(hd + 1) * VDIM]

    k = proj(1)
    put_cache_rows(kf_ref, k)
    v = proj(2)
    put_cache_rows(vf_ref, v)
    if emit_bf16_kv:
        put(kb_ref, k)
        put(vb_ref, v)
    put(sz_ref, _silu(proj(3)))

    b_gate = proj(4)
    u = proj(5) * proj(6)
    prev = carry[...]
    prev0 = jnp.broadcast_to(prev[:, 0:1, :], (g, r, D_CONV)).reshape(rows, D_CONV)
    prev1 = jnp.broadcast_to(prev[:, 1:2, :], (g, r, D_CONV)).reshape(rows, D_CONV)
    t = lax.broadcasted_iota(jnp.int32, (rows, D_CONV), 0) & (r - 1)
    u1 = jnp.where(t >= 1, pltpu.roll(u, 1, axis=0), prev1)
    u2 = jnp.where(t >= 2, pltpu.roll(u, 2, axis=0), jnp.where(t == 1, prev1, prev0))
    wc = wconv_ref[...]
    conv = wc[0:1, :] * u2 + wc[1:2, :] * u1 + wc[2:3, :] * u
    put(g_ref, b_gate * conv * _silu(proj(7)))

    last2 = u.reshape(g, r, D_CONV)[:, r - 2:, :]
    carry[...] = last2
    st_ref[...] = last2


def _premix(x3, mod3, w_in_bf, w_conv, prev_state, *, g, r, attn_dtype, emit_bf16_kv):
    n_groups, rows_per_group, _ = x3.shape
    grid = (n_groups // g, rows_per_group // r)
    assert r & (r - 1) == 0 and r >= 8
    act = lambda n: pl.BlockSpec((g, r, n), lambda o, i: (o, i, 0))
    mod_spec = lambda col: pl.BlockSpec((g, 1, D_MODEL), lambda o, i: (o, 0, col))
    sds = lambda dt: jax.ShapeDtypeStruct((n_groups, rows_per_group, D_ATTN), dt)
    cache_rows = jax.ShapeDtypeStruct((n_groups, rows_per_group * N_HEADS, VDIM), F32)
    out_shape = [sds(attn_dtype), sds(attn_dtype), cache_rows, cache_rows]
    if emit_bf16_kv:
        out_shape += [sds(BF16), sds(BF16)]
    out_shape += [sds(attn_dtype), sds(attn_dtype),
                  jax.ShapeDtypeStruct((n_groups, CONV_WIDTH - 1, D_CONV), F32)]
    out_specs = [act(D_ATTN)] * (len(out_shape) - 1)
    out_specs[2] = out_specs[3] = pl.BlockSpec((g, r * N_HEADS, VDIM), lambda o, i: (o, i, 0))
    out_specs.append(pl.BlockSpec((g, CONV_WIDTH - 1, D_CONV), lambda o, i: (o, 0, 0)))
    body = functools.partial(_premix_body, g=g, r=r, attn_dtype=attn_dtype,
                             emit_bf16_kv=emit_bf16_kv)
    return pl.pallas_call(
        body,
        out_shape=out_shape,
        grid=grid,
        in_specs=[act(D_MODEL), mod_spec(0), mod_spec(1),
                  pl.BlockSpec((D_MODEL, D_IN), lambda o, i: (0, 0)),
                  pl.BlockSpec((CONV_WIDTH, D_CONV), lambda o, i: (0, 0)),
                  pl.BlockSpec((g, CONV_WIDTH - 1, D_CONV), lambda o, i: (o, 0, 0))],
        out_specs=out_specs,
        scratch_shapes=[pltpu.VMEM((g, CONV_WIDTH - 1, D_CONV), F32)],
        compiler_params=pltpu.CompilerParams(dimension_semantics=("arbitrary", "arbitrary"),
                                             vmem_limit_bytes=VMEM_LIMIT),
        name="premix",
    )(x3, mod3, mod3, w_in_bf, w_conv, prev_state)


def _postmix_body(a_ref, g_ref, x_ref, gate_ref, w_ref, nf_ref, y_ref, *, g, r):
    rows = g * r
    a = a_ref[...].reshape(rows, D_ATTN).astype(BF16)
    gg = g_ref[...].reshape(rows, D_CONV).astype(BF16)
    mix = jnp.concatenate([a, gg], axis=1)
    out = jnp.dot(mix, w_ref[...], preferred_element_type=F32).reshape(g, r, D_MODEL)
    xo = x_ref[...] + gate_ref[...] * out
    y = xo * lax.rsqrt(jnp.mean(xo * xo, axis=-1, keepdims=True) + NORM_EPS)
    y_ref[...] = y * nf_ref[...]


def _postmix(a3, g3, x3, mod3, w_out_bf, norm_f, *, g, r):
    n_groups, rows_per_group, _ = x3.shape
    grid = (n_groups // g, rows_per_group // r)
    act = lambda n: pl.BlockSpec((g, r, n), lambda o, i: (o, i, 0))
    return pl.pallas_call(
        functools.partial(_postmix_body, g=g, r=r),
        out_shape=jax.ShapeDtypeStruct(x3.shape, F32),
        grid=grid,
        in_specs=[act(D_ATTN), act(D_CONV), act(D_MODEL),
                  pl.BlockSpec((g, 1, D_MODEL), lambda o, i: (o, 0, 2)),
                  pl.BlockSpec((D_ATTN + D_CONV, D_MODEL), lambda o, i: (0, 0)),
                  pl.BlockSpec((1, 1, D_MODEL), lambda o, i: (0, 0, 0))],
        out_specs=act(D_MODEL),
        compiler_params=pltpu.CompilerParams(dimension_semantics=("arbitrary", "arbitrary"),
                                             vmem_limit_bytes=VMEM_LIMIT),
        name="postmix",
    )(a3, g3, x3, mod3, w_out_bf, norm_f.reshape(1, 1, D_MODEL))


def _lambda(lq1_ref, lk1_ref, lq2_ref, lk2_ref, lam_init):
    l1 = jnp.sum(lq1_ref[...] * lk1_ref[...], axis=1, keepdims=True)
    l2 = jnp.sum(lq2_ref[...] * lk2_ref[...], axis=1, keepdims=True)
    return jnp.exp(l1) - jnp.exp(l2) + lam_init


def _tile_order(i, n):
    return jnp.where(i % 2 == 0, i // 2, n - 1 - i // 2)


def _sub_ln_gate(o, subln_ref, sz, lam_init):
    on = o * lax.rsqrt(jnp.mean(o * o, axis=-1, keepdims=True) + SUBLN_EPS)
    return on * (subln_ref[...] * (1.0 - lam_init)) * sz


def _attn_body(pt_ref,
               qa_ref, qb_ref, k_ref, v_ref, bias_ref, sz_ref,
               sqa_ref, sqb_ref, sqa_next_ref, sqb_next_ref, kn_ref, vn_ref, ssz_ref,
               hmask_ref, lastb_ref, newb_ref,
               lq1_ref, lk1_ref, lq2_ref, lk2_ref, subln_ref, k_hbm, v_hbm,
               o_ref, so_ref,
               s_scr, m_scr, acc_scr, bias_scr, k_buf, v_buf, sems, ss_even, ss_odd,
               sm_scr, sl_scr, sacc_scr,
               *, lam_init):
    order = pl.program_id(2)
    qi = _tile_order(order, pl.num_programs(2))
    step = (pl.program_id(0) * pl.num_programs(1) + pl.program_id(1)) * pl.num_programs(2) + order
    n_steps = pl.num_programs(0) * pl.num_programs(1) * pl.num_programs(2)
    lam = _lambda(lq1_ref, lk1_ref, lq2_ref, lk2_ref, lam_init)

    n_slots, n_pages = k_buf.shape[0], k_buf.shape[1]
    chunks = CHUNKS_PER_SAMPLE
    n_chunks = n_steps * chunks
    rows = sqa_ref.shape[1]

    def page_copies(hbm, buf, sem_row, chunk):
        slot = lax.rem(chunk, n_slots)
        return [pltpu.make_async_copy(hbm.at[pt_ref[chunk * n_pages + i]], buf.at[slot, i],
                                      sems.at[sem_row, slot]) for i in range(n_pages)]

    def k_copies(chunk):
        return page_copies(k_hbm, k_buf, 0, chunk)

    def v_copies(chunk):
        return page_copies(v_hbm, v_buf, 1, chunk)

    def start_chunk(chunk):
        for cp in k_copies(chunk) + v_copies(chunk):
            cp.start()

    def head_slice(h):
        return slice(h * VDIM, (h + 1) * VDIM)

    def stack_queries(a_ref, b_ref):
        return jnp.concatenate([r[0, :, head_slice(h)] for h in range(N_HEADS)
                                for r in (a_ref, b_ref)], axis=0).astype(BF16)

    def page_scores(q_all, chunk):
        slot = lax.rem(chunk, n_slots)
        return jnp.concatenate([_dot_nt(q_all, k_buf[slot, i].astype(BF16))
                                for i in range(n_pages)], axis=1)

    sq_all = stack_queries(sqa_ref, sqb_ref)
    sq_next = stack_queries(sqa_next_ref, sqb_next_ref)

    @pl.when(step == 0)
    def _():
        for ahead in range(n_slots - 1):
            start_chunk(ahead)
        for cp in k_copies(0):
            cp.wait()
        ss_even[...] = page_scores(sq_all, 0)

    sm_scr[...] = jnp.full(sm_scr.shape, NEG_INF, F32)
    sl_scr[...] = jnp.zeros(sl_scr.shape, F32)
    sacc_scr[...] = jnp.zeros(sacc_scr.shape, F32)

    def sample_update(s, weighted_values):
        m_prev = sm_scr[...]
        m_new = jnp.maximum(m_prev, jnp.max(s, axis=1, keepdims=True))
        alpha = jnp.exp2(m_prev - m_new)
        p = jnp.exp2(s - _lane_tile(m_new, s.shape[1] // LANES))
        sl_scr[...] = alpha * sl_scr[...] + jnp.sum(p, axis=1, keepdims=True)
        sacc_scr[...] = alpha * sacc_scr[...] + weighted_values(p.astype(BF16))
        sm_scr[...] = m_new

    def sample_chunk(j, s_ref, s_next_ref):
        chunk = step * chunks + j
        slot = lax.rem(chunk, n_slots)
        ahead = chunk + (n_slots - 1)
        nxt = chunk + 1

        @pl.when(ahead < n_chunks)
        def _():
            start_chunk(ahead)

        @pl.when(nxt < n_chunks)
        def _():
            for cp in k_copies(nxt):
                cp.wait()

        for cp in v_copies(chunk):
            cp.wait()
        q_next = jnp.where(j == chunks - 1, sq_next, sq_all)
        s_next = page_scores(q_next, jnp.minimum(nxt, n_chunks - 1))
        hmask = hmask_ref[...]
        last = jnp.where(j == chunks - 1, lastb_ref[...], hmask)
        s = s_ref[...] + jnp.concatenate([hmask] * (n_pages - 1) + [last], axis=1)
        width = hmask.shape[1]

        def weighted_values(pb):
            pv = None
            for i in range(n_pages):
                d = jnp.dot(pb[:, i * width:(i + 1) * width], v_buf[slot, i].astype(BF16),
                            preferred_element_type=F32)
                pv = d if pv is None else pv + d
            return pv

        sample_update(s, weighted_values)
        s_next_ref[...] = s_next

    q2 = jnp.concatenate([qa_ref[0], qb_ref[0]], axis=0)
    m_scr[...] = jnp.full(m_scr.shape, NEG_INF, F32)
    acc_scr[...] = jnp.zeros(acc_scr.shape, F32)

    def scores(j):
        return _dot_nt(q2, k_ref[0, pl.ds(pl.multiple_of(j * TK, TK), TK), :])

    def absorb(s, j):
        v_t = v_ref[0, pl.ds(pl.multiple_of(j * TK, TK), TK), :]
        m_prev = m_scr[...]
        m_new = jnp.maximum(m_prev, jnp.max(s, axis=1, keepdims=True))
        alpha = jnp.exp2(m_prev - m_new)
        p = jnp.exp2((s - _lane_tile(m_new, TK // LANES)).astype(BF16))
        v_aug = jnp.concatenate([v_t, jnp.ones((TK, LANES), BF16)], axis=1)
        pv = jnp.dot(p, v_aug, preferred_element_type=F32)
        acc_scr[...] = acc_scr[...] * _lane_tile(alpha, 2) + pv
        m_scr[...] = m_new

    n_far = jnp.maximum(qi * (TQ // TK) - 1, 0)
    s_scr[...] = scores(0)

    @pl.when(order <= 1)
    def _():
        row = jnp.where(order == 0, bias_ref[0, 0:1, :], bias_ref[0, 1:2, :])
        skewed = pltpu.roll(jnp.broadcast_to(row, (TQ, row.shape[1])), 0, 1,
                            stride=1, stride_axis=0)
        bias_scr[...] = skewed[:, TQ:TQ + NEAR_TILES * TK]

    def far_tiles(j0, count):
        s = s_scr[...]
        for u in range(count):
            s_next = scores(j0 + u + 1)
            absorb(s, j0 + u)
            s = s_next
        s_scr[...] = s

    def far_group(jj, carry):
        far_tiles(jj * FAR_UNROLL, FAR_UNROLL)
        return carry

    def far_single(j, carry):
        far_tiles(j, 1)
        return carry

    n_groups = n_far // FAR_UNROLL

    def interleaved(jj, carry):
        for parity, (s_ref, s_next_ref) in enumerate(((ss_even, ss_odd), (ss_odd, ss_even))):
            j = 2 * jj + parity
            sample_chunk(j, s_ref, s_next_ref)
            lax.fori_loop((n_groups * j) // chunks, (n_groups * (j + 1)) // chunks, far_group, 0)
        return carry

    lax.fori_loop(0, chunks // 2, interleaved, 0)
    lax.fori_loop(n_groups * FAR_UNROLL, n_far, far_single, 0)
    s = s_scr[...]
    for u in range(NEAR_TILES):
        s_next = scores(n_far + u + 1) if u + 1 < NEAR_TILES else None
        absorb(s + _row_tile(bias_scr[:, u * TK:(u + 1) * TK]), n_far + u)
        s = s_next

    acc = acc_scr[...]
    o_all = acc[:, :VDIM] / acc[:, VDIM:]
    o = o_all[:TQ] - lam * o_all[TQ:]
    o_ref[0] = _sub_ln_gate(o, subln_ref, sz_ref[0].astype(F32), lam_init).astype(o_ref.dtype)

    def new_page(ref):
        pad = jnp.zeros((LANES - N_HEADS * rows, VDIM), F32)
        return jnp.concatenate([ref[0], pad], axis=0).astype(BF16)

    sample_update(_dot_nt(sq_all, new_page(kn_ref)) + newb_ref[...],
                  lambda pb: jnp.dot(pb, new_page(vn_ref), preferred_element_type=F32))
    so_all = sacc_scr[...] / sl_scr[...]
    for h in range(N_HEADS):
        r0 = 2 * rows * h
        so = so_all[r0:r0 + rows] - lam * so_all[r0 + rows:r0 + 2 * rows]
        so_ref[0, :, head_slice(h)] = _sub_ln_gate(so, subln_ref, ssz_ref[0, :, head_slice(h)],
                                                   lam_init)


def _attention(page_table, qa, qb, kb, vb, bias, sz, sqa, sqb, kn, vn, ssz, hmask, lastb, newb,
               lams, subln, cache_k, cache_v, *, lam_init):
    b, t, _ = qa.shape
    nb, rows, _ = sqa.shape
    n_phys, page_rows, _ = cache_k.shape
    nq = t // TQ
    q_rows = 2 * rows * N_HEADS
    pages_per_chunk = page_table.shape[1] // CHUNKS_PER_SAMPLE
    assert TQ % TK == 0 and t % TQ == 0
    assert b * N_HEADS * nq == nb, "one sample per prompt query tile"
    assert page_table.shape[1] % CHUNKS_PER_SAMPLE == 0 and CHUNKS_PER_SAMPLE % 2 == 0

    def flat(bi, h, qi):
        return (bi * N_HEADS + h) * nq + qi

    qspec = pl.BlockSpec((1, TQ, VDIM), lambda bi, h, qi, pt: (bi, _tile_order(qi, nq), h))
    kvspec = pl.BlockSpec((1, t, VDIM), lambda bi, h, qi, pt: (bi, 0, h))
    sspec = pl.BlockSpec((1, rows, D_ATTN), lambda bi, h, qi, pt: (flat(bi, h, qi), 0, 0))
    nspec = pl.BlockSpec((1, rows * N_HEADS, VDIM), lambda bi, h, qi, pt: (flat(bi, h, qi), 0, 0))
    const = lambda *shape: pl.BlockSpec(shape, lambda bi, h, qi, pt: (0,) * len(shape))
    hbm = pl.BlockSpec(memory_space=pl.ANY)
    page_buf = pltpu.VMEM((PAGE_SLOTS, pages_per_chunk, page_rows, VDIM), F32)
    score_buf = pltpu.VMEM((q_rows, pages_per_chunk * page_rows), F32)
    sspec_next = pl.BlockSpec(
        (1, rows, D_ATTN), lambda bi, h, qi, pt: (jnp.minimum(flat(bi, h, qi) + 1, nb - 1), 0, 0))
    grid_spec = pltpu.PrefetchScalarGridSpec(
        num_scalar_prefetch=1,
        grid=(b, N_HEADS, nq),
        in_specs=[qspec, qspec, kvspec, kvspec,
                  pl.BlockSpec((1, 2, TQ + NEAR_TILES * TK), lambda bi, h, qi, pt: (h, 0, 0)),
                  qspec,
                  sspec, sspec, sspec_next, sspec_next, nspec, nspec, sspec,
                  const(q_rows, page_rows), const(q_rows, page_rows), const(q_rows, LANES),
                  const(1, HALF_DIM), const(1, HALF_DIM), const(1, HALF_DIM), const(1, HALF_DIM),
                  const(1, VDIM), hbm, hbm],
        out_specs=[qspec, sspec],
        scratch_shapes=[pltpu.VMEM((2 * TQ, TK), F32), pltpu.VMEM((2 * TQ, LANES), F32),
                        pltpu.VMEM((2 * TQ, 2 * LANES), F32),
                        pltpu.VMEM((TQ, NEAR_TILES * TK), F32),
                        page_buf, page_buf, pltpu.SemaphoreType.DMA((2, PAGE_SLOTS)),
                        score_buf, score_buf]
                       + [pltpu.VMEM((q_rows, LANES), F32)] * 3,
    )
    return pl.pallas_call(
        functools.partial(_attn_body, lam_init=lam_init),
        out_shape=[jax.ShapeDtypeStruct((b, t, D_ATTN), BF16),
                   jax.ShapeDtypeStruct((nb, rows, D_ATTN), F32)],
        grid_spec=grid_spec,
        compiler_params=pltpu.CompilerParams(
            dimension_semantics=("arbitrary", "arbitrary", "arbitrary"),
            vmem_limit_bytes=VMEM_LIMIT),
        name="attention",
    )(page_table.reshape(-1), qa, qb, kb, vb, bias, sz, sqa, sqb, sqa, sqb, kn, vn, ssz,
      hmask, lastb, newb,
      *lams, subln, cache_k, cache_v)


def _bias_by_distance(rel_bias, d):
    max_exact = N_BUCKETS // 2
    n = jnp.maximum(d, 0)
    n_f = jnp.maximum(n, 1).astype(F32)
    large = max_exact + (jnp.log(n_f / max_exact) / math.log(MAX_DISTANCE / max_exact)
                         * (N_BUCKETS - max_exact)).astype(jnp.int32)
    bucket = jnp.where(n < max_exact, n, jnp.minimum(large, N_BUCKETS - 1))
    table = rel_bias.astype(F32)
    vals = (table[bucket] - table[N_BUCKETS - 1]).T * LOG2_E
    return jnp.where(d[None, :] >= 0, vals, NEG_INF)


def _toeplitz(rel_bias, delta, n_q, n_k):
    w = n_q + n_k
    row = _bias_by_distance(rel_bias, delta + (n_q - 1) - jnp.arange(w))
    skew = jnp.tile(row, (1, n_q))[:, :n_q * (w - 1)].reshape(N_HEADS, n_q, w - 1)
    return skew[:, :, n_q - 1:n_q - 1 + n_k]


def kernel(x_prompt, x_sample, cache_k, cache_v, state_conv, page_table, c_prompt, c_sample,
           rel_bias, w_ada, b_ada, w_in, w_conv, lambda_q1, lambda_k1, lambda_q2, lambda_k2,
           subln_w, w_out, norm_f):
    depth = w_in.shape[0]
    assert depth == 1
    batch, seq, _ = x_prompt.shape
    nb, dec_seq, _ = x_sample.shape
    n_phys, page = cache_k.shape[1], cache_k.shape[2]
    lam_init = 0.8 - 0.6 * math.exp(-0.3 * 0)

    n_c = batch + nb
    n_c_pad = -(-n_c // MOD_ROWS_PAD) * MOD_ROWS_PAD
    c_all = jnp.concatenate([c_prompt, c_sample, jnp.zeros((n_c_pad - n_c, D_MODEL), F32)], axis=0)
    mod = _modulation(c_all, w_ada[0], b_ada[0].reshape(1, -1))
    mod_p = mod[:batch].reshape(batch, 1, 3 * D_MODEL)
    mod_s = mod[batch:n_c].reshape(nb, 1, 3 * D_MODEL)

    w_in_bf = w_in[0].astype(BF16)
    w_out_bf = w_out[0].astype(BF16)
    lams = [a.reshape(1, HALF_DIM) for a in (lambda_q1[0], lambda_k1[0], lambda_q2[0], lambda_k2[0])]
    subln = subln_w[0].reshape(1, VDIM)

    back = TQ - jnp.arange(TQ + NEAR_TILES * TK)
    bias_prompt = jnp.stack([_bias_by_distance(rel_bias, back),
                             _bias_by_distance(rel_bias, TK + back)], axis=1)

    q_rows = 2 * dec_seq * N_HEADS
    page_rows = page * N_HEADS
    row_head = jnp.arange(q_rows) // (2 * dec_seq)
    same_head = row_head[:, None] == (jnp.arange(page_rows) % N_HEADS)[None, :]
    head_mask = jnp.where(same_head, 0.0, NEG_INF).astype(F32)
    both_maps = lambda t: jnp.concatenate([t, t], axis=1).reshape(q_rows, t.shape[2])
    t_last = both_maps(_toeplitz(rel_bias, page, dec_seq, page))
    last_page = jnp.where(same_head, jnp.repeat(t_last, N_HEADS, axis=1), NEG_INF)
    t_new = both_maps(_toeplitz(rel_bias, 0, dec_seq, dec_seq))
    new_cols = dec_seq * N_HEADS
    bias_new = jnp.where(same_head[:, :new_cols], jnp.repeat(t_new, N_HEADS, axis=1), NEG_INF)
    bias_new = jnp.concatenate([bias_new, jnp.full((q_rows, LANES - new_cols), NEG_INF, F32)],
                               axis=1)

    qa, qb, kf, vf, kb, vb, sz, gp, st_p = _premix(
        x_prompt, mod_p, w_in_bf, w_conv[0], jnp.zeros((batch, CONV_WIDTH - 1, D_CONV), F32),
        g=1, r=ROW_TILE, attn_dtype=BF16, emit_bf16_kv=True)
    g_s = ROW_TILE // dec_seq
    qa_s, qb_s, kf_s, vf_s, sz_s, gs, st_s = _premix(
        x_sample, mod_s, w_in_bf, w_conv[0], state_conv[0],
        g=g_s, r=dec_seq, attn_dtype=F32, emit_bf16_kv=False)
    pages_k = cache_k.reshape(n_phys, page_rows, VDIM)
    pages_v = cache_v.reshape(n_phys, page_rows, VDIM)
    a_p, a_s = _attention(page_table, qa, qb, kb, vb, bias_prompt, sz, qa_s, qb_s, kf_s, vf_s, sz_s,
                          head_mask, last_page, bias_new, lams, subln, pages_k, pages_v,
                          lam_init=lam_init)
    y_prompt = _postmix(a_p, gp, x_prompt, mod_p, w_out_bf, norm_f, g=1, r=ROW_TILE)
    y_sample = _postmix(a_s, gs, x_sample, mod_s, w_out_bf, norm_f, g=g_s, r=dec_seq)

    kv_p = lambda a: a.reshape(1, batch, seq, N_HEADS, VDIM)
    kv_s = lambda a: a.reshape(1, nb, dec_seq, N_HEADS, VDIM)
    return (y_prompt, y_sample, kv_p(kf), kv_p(vf), st_p[None], kv_s(kf_s), kv_s(vf_s), st_s[None])
```

```python
import functools
import math

import jax
import jax.numpy as jnp
from jax import lax
from jax.experimental import pallas as pl
from jax.experimental.pallas import tpu as pltpu

F32 = jnp.float32
BF16 = jnp.bfloat16

D_MODEL = 1024
D_ATTN = 512
D_CONV = 512
HALF_DIM = 64
VDIM = 128
N_HEADS = 4
CONV_WIDTH = 3
N_BUCKETS = 32
MAX_DISTANCE = 128
NORM_EPS = 1e-6
SUBLN_EPS = 1e-5
ATTN_SCALE = HALF_DIM ** -0.5
LOG2_E = math.log2(math.e)
NEG_INF = -1e30
D_IN = 4 * D_ATTN + 4 * D_CONV

LANES = 128
VMEM_LIMIT = 56 * 1024 * 1024

MOD_ROWS_PAD = 8
MOD_TN = 512
ROW_TILE = 512
TQ = 512
TK = 256
NEAR_TILES = TQ // TK + 1
FAR_UNROLL = 4
CHUNKS_PER_SAMPLE = 8
PAGE_SLOTS = 8


def _silu(x):
    return x * jax.nn.sigmoid(x)


def _dot_nt(a, b):
    return lax.dot_general(a, b, (((1,), (1,)), ((), ())), preferred_element_type=F32)


def _lane_tile(x, n):
    return x if n == 1 else jnp.concatenate([x] * n, axis=1)


def _row_tile(x):
    return jnp.concatenate([x, x], axis=0)


def _mod_body(c_ref, w_ref, b_ref, o_ref):
    c = c_ref[...]
    o_ref[...] = jnp.dot(_silu(c).astype(BF16), w_ref[...].astype(BF16),
                         preferred_element_type=F32) + b_ref[...]


def _modulation(c_all, w_ada, b_ada):
    rows = c_all.shape[0]
    n = w_ada.shape[1]
    return pl.pallas_call(
        _mod_body,
        out_shape=jax.ShapeDtypeStruct((rows, n), F32),
        grid=(n // MOD_TN,),
        in_specs=[pl.BlockSpec((rows, D_MODEL), lambda j: (0, 0)),
                  pl.BlockSpec((D_MODEL, MOD_TN), lambda j: (0, j)),
                  pl.BlockSpec((1, MOD_TN), lambda j: (0, j))],
        out_specs=pl.BlockSpec((rows, MOD_TN), lambda j: (0, j)),
        compiler_params=pltpu.CompilerParams(dimension_semantics=("arbitrary",),
                                             vmem_limit_bytes=VMEM_LIMIT),
        name="adaln_mod",
    )(c_all, w_ada, b_ada)


def _premix_body(x_ref, shift_ref, scale_ref, w_ref, wconv_ref, prev_ref, *rest, g, r, attn_dtype,
                 emit_bf16_kv):
    if emit_bf16_kv:
        qa_ref, qb_ref, kf_ref, vf_ref, kb_ref, vb_ref, sz_ref, g_ref, st_ref, carry = rest
    else:
        qa_ref, qb_ref, kf_ref, vf_ref, sz_ref, g_ref, st_ref, carry = rest
    rows = g * r

    @pl.when(pl.program_id(1) == 0)
    def _():
        carry[...] = prev_ref[...]

    x = x_ref[...]
    xn = x * lax.rsqrt(jnp.mean(x * x, axis=-1, keepdims=True) + NORM_EPS)
    h = xn * (1.0 + scale_ref[...]) + shift_ref[...]
    h2 = h.reshape(rows, D_MODEL).astype(BF16)

    def proj(i):
        return jnp.dot(h2, w_ref[:, i * D_ATTN:(i + 1) * D_ATTN], preferred_element_type=F32)

    def put(ref, val):
        ref[...] = val.reshape(g, r, val.shape[-1]).astype(ref.dtype)

    q = proj(0) * (ATTN_SCALE * LOG2_E)
    lane = lax.broadcasted_iota(jnp.int32, (rows, D_ATTN), 1)
    first_half = (lane & (VDIM - 1)) < HALF_DIM
    put(qa_ref, jnp.where(first_half, q, 0.0))
    put(qb_ref, jnp.where(first_half, 0.0, q))
    def put_cache_rows(ref, val):
        val3 = val.reshape(g, r, D_ATTN)
        for hd in range(N_HEADS):
            ref[:, pl.ds(hd, r, stride=N_HEADS), :] = val3[:, :, hd * VDIM:(hd + 1) * VDIM]

    k = proj(1)
    put_cache_rows(kf_ref, k)
    v = proj(2)
    put_cache_rows(vf_ref, v)
    if emit_bf16_kv:
        put(kb_ref, k)
        put(vb_ref, v)
    put(sz_ref, _silu(proj(3)))

    b_gate = proj(4)
    u = proj(5) * proj(6)
    prev = carry[...]
    prev0 = jnp.broadcast_to(prev[:, 0:1, :], (g, r, D_CONV)).reshape(rows, D_CONV)
    prev1 = jnp.broadcast_to(prev[:, 1:2, :], (g, r, D_CONV)).reshape(rows, D_CONV)
    t = lax.broadcasted_iota(jnp.int32, (rows, D_CONV), 0) & (r - 1)
    u1 = jnp.where(t >= 1, pltpu.roll(u, 1, axis=0), prev1)
    u2 = jnp.where(t >= 2, pltpu.roll(u, 2, axis=0), jnp.where(t == 1, prev1, prev0))
    wc = wconv_ref[...]
    conv = wc[0:1, :] * u2 + wc[1:2, :] * u1 + wc[2:3, :] * u
    put(g_ref, b_gate * conv * _silu(proj(7)))

    last2 = u.reshape(g, r, D_CONV)[:, r - 2:, :]
    carry[...] = last2
    st_ref[...] = last2


def _premix(x3, mod3, w_in_bf, w_conv, prev_state, *, g, r, attn_dtype, emit_bf16_kv):
    n_groups, rows_per_group, _ = x3.shape
    grid = (n_groups // g, rows_per_group // r)
    assert r & (r - 1) == 0 and r >= 8
    act = lambda n: pl.BlockSpec((g, r, n), lambda o, i: (o, i, 0))
    mod_spec = lambda col: pl.BlockSpec((g, 1, D_MODEL), lambda o, i: (o, 0, col))
    sds = lambda dt: jax.ShapeDtypeStruct((n_groups, rows_per_group, D_ATTN), dt)
    cache_rows = jax.ShapeDtypeStruct((n_groups, rows_per_group * N_HEADS, VDIM), F32)
    out_shape = [sds(attn_dtype), sds(attn_dtype), cache_rows, cache_rows]
    if emit_bf16_kv:
        out_shape += [sds(BF16), sds(BF16)]
    out_shape += [sds(attn_dtype), sds(attn_dtype),
                  jax.ShapeDtypeStruct((n_groups, CONV_WIDTH - 1, D_CONV), F32)]
    out_specs = [act(D_ATTN)] * (len(out_shape) - 1)
    out_specs[2] = out_specs[3] = pl.BlockSpec((g, r * N_HEADS, VDIM), lambda o, i: (o, i, 0))
    out_specs.append(pl.BlockSpec((g, CONV_WIDTH - 1, D_CONV), lambda o, i: (o, 0, 0)))
    body = functools.partial(_premix_body, g=g, r=r, attn_dtype=attn_dtype,
                             emit_bf16_kv=emit_bf16_kv)
    return pl.pallas_call(
        body,
        out_shape=out_shape,
        grid=grid,
        in_specs=[act(D_MODEL), mod_spec(0), mod_spec(1),
                  pl.BlockSpec((D_MODEL, D_IN), lambda o, i: (0, 0)),
                  pl.BlockSpec((CONV_WIDTH, D_CONV), lambda o, i: (0, 0)),
                  pl.BlockSpec((g, CONV_WIDTH - 1, D_CONV), lambda o, i: (o, 0, 0))],
        out_specs=out_specs,
        scratch_shapes=[pltpu.VMEM((g, CONV_WIDTH - 1, D_CONV), F32)],
        compiler_params=pltpu.CompilerParams(dimension_semantics=("arbitrary", "arbitrary"),
                                             vmem_limit_bytes=VMEM_LIMIT),
        name="premix",
    )(x3, mod3, mod3, w_in_bf, w_conv, prev_state)


def _postmix_body(a_ref, g_ref, x_ref, gate_ref, w_ref, nf_ref, y_ref, *, g, r):
    rows = g * r
    a = a_ref[...].reshape(rows, D_ATTN).astype(BF16)
    gg = g_ref[...].reshape(rows, D_CONV).astype(BF16)
    mix = jnp.concatenate([a, gg], axis=1)
    out = jnp.dot(mix, w_ref[...], preferred_element_type=F32).reshape(g, r, D_MODEL)
    xo = x_ref[...] + gate_ref[...] * out
    y = xo * lax.rsqrt(jnp.mean(xo * xo, axis=-1, keepdims=True) + NORM_EPS)
    y_ref[...] = y * nf_ref[...]


def _postmix(a3, g3, x3, mod3, w_out_bf, norm_f, *, g, r):
    n_groups, rows_per_group, _ = x3.shape
    grid = (n_groups // g, rows_per_group // r)
    act = lambda n: pl.BlockSpec((g, r, n), lambda o, i: (o, i, 0))
    return pl.pallas_call(
        functools.partial(_postmix_body, g=g, r=r),
        out_shape=jax.ShapeDtypeStruct(x3.shape, F32),
        grid=grid,
        in_specs=[act(D_ATTN), act(D_CONV), act(D_MODEL),
                  pl.BlockSpec((g, 1, D_MODEL), lambda o, i: (o, 0, 2)),
                  pl.BlockSpec((D_ATTN + D_CONV, D_MODEL), lambda o, i: (0, 0)),
                  pl.BlockSpec((1, 1, D_MODEL), lambda o, i: (0, 0, 0))],
        out_specs=act(D_MODEL),
        compiler_params=pltpu.CompilerParams(dimension_semantics=("arbitrary", "arbitrary"),
                                             vmem_limit_bytes=VMEM_LIMIT),
        name="postmix",
    )(a3, g3, x3, mod3, w_out_bf, norm_f.reshape(1, 1, D_MODEL))


def _lambda(lq1_ref, lk1_ref, lq2_ref, lk2_ref, lam_init):
    l1 = jnp.sum(lq1_ref[...] * lk1_ref[...], axis=1, keepdims=True)
    l2 = jnp.sum(lq2_ref[...] * lk2_ref[...], axis=1, keepdims=True)
    return jnp.exp(l1) - jnp.exp(l2) + lam_init


def _tile_order(i, n):
    return jnp.where(i % 2 == 0, i // 2, n - 1 - i // 2)


def _sub_ln_gate(o, subln_ref, sz, lam_init):
    on = o * lax.rsqrt(jnp.mean(o * o, axis=-1, keepdims=True) + SUBLN_EPS)
    return on * (subln_ref[...] * (1.0 - lam_init)) * sz


def _attn_body(pt_ref,
               qa_ref, qb_ref, k_ref, v_ref, bias_ref, sz_ref,
               sqa_ref, sqb_ref, sqa_next_ref, sqb_next_ref, kn_ref, vn_ref, ssz_ref,
               hmask_ref, lastb_ref, newb_ref,
               lq1_ref, lk1_ref, lq2_ref, lk2_ref, subln_ref, k_hbm, v_hbm,
               o_ref, so_ref,
               s_scr, m_scr, acc_scr, bias_scr, k_buf, v_buf, sems, ss_even, ss_odd,
               sm_scr, sl_scr, sacc_scr,
               *, lam_init):
    order = pl.program_id(2)
    qi = _tile_order(order, pl.num_programs(2))
    step = (pl.program_id(0) * pl.num_programs(1) + pl.program_id(1)) * pl.num_programs(2) + order
    n_steps = pl.num_programs(0) * pl.num_programs(1) * pl.num_programs(2)
    lam = _lambda(lq1_ref, lk1_ref, lq2_ref, lk2_ref, lam_init)

    n_slots, n_pages = k_buf.shape[0], k_buf.shape[1]
    chunks = CHUNKS_PER_SAMPLE
    n_chunks = n_steps * chunks
    rows = sqa_ref.shape[1]

    def page_copies(hbm, buf, sem_row, chunk):
        slot = lax.rem(chunk, n_slots)
        return [pltpu.make_async_copy(hbm.at[pt_ref[chunk * n_pages + i]], buf.at[slot, i],
                                      sems.at[sem_row, slot]) for i in range(n_pages)]

    def k_copies(chunk):
        return page_copies(k_hbm, k_buf, 0, chunk)

    def v_copies(chunk):
        return page_copies(v_hbm, v_buf, 1, chunk)

    def start_chunk(chunk):
        for cp in k_copies(chunk) + v_copies(chunk):
            cp.start()

    def head_slice(h):
        return slice(h * VDIM, (h + 1) * VDIM)

    def stack_queries(a_ref, b_ref):
        return jnp.concatenate([r[0, :, head_slice(h)] for h in range(N_HEADS)
                                for r in (a_ref, b_ref)], axis=0).astype(BF16)

    def page_scores(q_all, chunk):
        slot = lax.rem(chunk, n_slots)
        return jnp.concatenate([_dot_nt(q_all, k_buf[slot, i].astype(BF16))
                                for i in range(n_pages)], axis=1)

    sq_all = stack_queries(sqa_ref, sqb_ref)
    sq_next = stack_queries(sqa_next_ref, sqb_next_ref)

    @pl.when(step == 0)
    def _():
        for ahead in range(n_slots - 1):
            start_chunk(ahead)
        for cp in k_copies(0):
            cp.wait()
        ss_even[...] = page_scores(sq_all, 0)

    sm_scr[...] = jnp.full(sm_scr.shape, NEG_INF, F32)
    sl_scr[...] = jnp.zeros(sl_scr.shape, F32)
    sacc_scr[...] = jnp.zeros(sacc_scr.shape, F32)

    def sample_update(s, weighted_values):
        m_prev = sm_scr[...]
        m_new = jnp.maximum(m_prev, jnp.max(s, axis=1, keepdims=True))
        alpha = jnp.exp2(m_prev - m_new)
        p = jnp.exp2(s - _lane_tile(m_new, s.shape[1] // LANES))
        sl_scr[...] = alpha * sl_scr[...] + jnp.sum(p, axis=1, keepdims=True)
        sacc_scr[...] = alpha * sacc_scr[...] + weighted_values(p.astype(BF16))
        sm_scr[...] = m_new

    def chunk_transfers(j):
        chunk = step * chunks + j
        ahead = chunk + (n_slots - 1)
        nxt = chunk + 1

        @pl.when(ahead < n_chunks)
        def _():
            start_chunk(ahead)

        @pl.when(nxt < n_chunks)
        def _():
            for cp in k_copies(nxt):
                cp.wait()

        for cp in v_copies(chunk):
            cp.wait()

    def sample_chunk(j, s_ref, s_next_ref):
        chunk = step * chunks + j
        slot = lax.rem(chunk, n_slots)
        nxt = chunk + 1
        q_next = jnp.where(j == chunks - 1, sq_next, sq_all)
        s_next = page_scores(q_next, jnp.minimum(nxt, n_chunks - 1))
        width = lastb_ref.shape[1]
        hmask = _lane_tile(hmask_ref[...], width // LANES)
        last = jnp.where(j == chunks - 1, lastb_ref[...], hmask)
        s = s_ref[...] + jnp.concatenate([hmask] * (n_pages - 1) + [last], axis=1)

        def weighted_values(pb):
            v_all = jnp.concatenate([v_buf[slot, i].astype(BF16) for i in range(n_pages)], axis=0)
            return jnp.dot(pb, v_all, preferred_element_type=F32)

        sample_update(s, weighted_values)
        s_next_ref[...] = s_next

    q2 = jnp.concatenate([qa_ref[0], qb_ref[0]], axis=0)
    m_scr[...] = jnp.full(m_scr.shape, NEG_INF, F32)
    acc_scr[...] = jnp.zeros(acc_scr.shape, F32)

    def scores(j):
        return _dot_nt(q2, k_ref[0, pl.ds(pl.multiple_of(j * TK, TK), TK), :])

    def absorb(s, j):
        v_t = v_ref[0, pl.ds(pl.multiple_of(j * TK, TK), TK), :]
        m_prev = m_scr[...]
        m_new = jnp.maximum(m_prev, jnp.max(s, axis=1, keepdims=True))
        alpha = jnp.exp2(m_prev - m_new)
        p = jnp.exp2((s - _lane_tile(m_new, TK // LANES)).astype(BF16))
        v_aug = jnp.concatenate([v_t, jnp.ones((TK, LANES), BF16)], axis=1)
        pv = jnp.dot(p, v_aug, preferred_element_type=F32)
        acc_scr[...] = acc_scr[...] * _lane_tile(alpha, 2) + pv
        m_scr[...] = m_new

    n_far = jnp.maximum(qi * (TQ // TK) - 1, 0)
    s_scr[...] = scores(0)

    @pl.when(order <= 1)
    def _():
        row = jnp.where(order == 0, bias_ref[0, 0:1, :], bias_ref[0, 1:2, :])
        skewed = pltpu.roll(jnp.broadcast_to(row, (TQ, row.shape[1])), 0, 1,
                            stride=1, stride_axis=0)
        bias_scr[...] = skewed[:, TQ:TQ + NEAR_TILES * TK]

    def far_tiles(j0, count):
        s = s_scr[...]
        for u in range(count):
            s_next = scores(j0 + u + 1)
            absorb(s, j0 + u)
            s = s_next
        s_scr[...] = s

    def far_single(j, carry):
        far_tiles(j, 1)
        return carry

    n_groups = n_far // FAR_UNROLL

    def interleaved(jj, carry):
        for parity, (s_ref, s_next_ref) in enumerate(((ss_even, ss_odd), (ss_odd, ss_even))):
            j = 2 * jj + parity
            chunk_transfers(j)
            group = (n_groups * j) // chunks
            has_group = (n_groups * (j + 1)) // chunks > group

            @pl.when(has_group)
            def _():
                sample_chunk(j, s_ref, s_next_ref)
                far_tiles(group * FAR_UNROLL, FAR_UNROLL)

            @pl.when(jnp.logical_not(has_group))
            def _():
                sample_chunk(j, s_ref, s_next_ref)
        return carry

    lax.fori_loop(0, chunks // 2, interleaved, 0)
    lax.fori_loop(n_groups * FAR_UNROLL, n_far, far_single, 0)
    s = s_scr[...]
    for u in range(NEAR_TILES):
        s_next = scores(n_far + u + 1) if u + 1 < NEAR_TILES else None
        absorb(s + _row_tile(bias_scr[:, u * TK:(u + 1) * TK]), n_far + u)
        s = s_next

    acc = acc_scr[...]
    o_all = acc[:, :VDIM] / acc[:, VDIM:]
    o = o_all[:TQ] - lam * o_all[TQ:]
    o_ref[0] = _sub_ln_gate(o, subln_ref, sz_ref[0].astype(F32), lam_init).astype(o_ref.dtype)

    def new_page(ref):
        pad = jnp.zeros((LANES - N_HEADS * rows, VDIM), F32)
        return jnp.concatenate([ref[0], pad], axis=0).astype(BF16)

    sample_update(_dot_nt(sq_all, new_page(kn_ref)) + newb_ref[...],
                  lambda pb: jnp.dot(pb, new_page(vn_ref), preferred_element_type=F32))
    so_all = sacc_scr[...] / sl_scr[...]
    for h in range(N_HEADS):
        r0 = 2 * rows * h
        so = so_all[r0:r0 + rows] - lam * so_all[r0 + rows:r0 + 2 * rows]
        so_ref[0, :, head_slice(h)] = _sub_ln_gate(so, subln_ref, ssz_ref[0, :, head_slice(h)],
                                                   lam_init)


def _attention(page_table, qa, qb, kb, vb, bias, sz, sqa, sqb, kn, vn, ssz, hmask, lastb, newb,
               lams, subln, cache_k, cache_v, *, lam_init):
    b, t, _ = qa.shape
    nb, rows, _ = sqa.shape
    n_phys, page_rows, _ = cache_k.shape
    nq = t // TQ
    q_rows = 2 * rows * N_HEADS
    pages_per_chunk = page_table.shape[1] // CHUNKS_PER_SAMPLE
    assert TQ % TK == 0 and t % TQ == 0
    assert b * N_HEADS * nq == nb, "one sample per prompt query tile"
    assert page_table.shape[1] % CHUNKS_PER_SAMPLE == 0 and CHUNKS_PER_SAMPLE % 2 == 0
    assert (nq * (TQ // TK) - 1) // FAR_UNROLL <= CHUNKS_PER_SAMPLE

    def flat(bi, h, qi):
        return (bi * N_HEADS + h) * nq + qi

    qspec = pl.BlockSpec((1, TQ, VDIM), lambda bi, h, qi, pt: (bi, _tile_order(qi, nq), h))
    kvspec = pl.BlockSpec((1, t, VDIM), lambda bi, h, qi, pt: (bi, 0, h))
    sspec = pl.BlockSpec((1, rows, D_ATTN), lambda bi, h, qi, pt: (flat(bi, h, qi), 0, 0))
    nspec = pl.BlockSpec((1, rows * N_HEADS, VDIM), lambda bi, h, qi, pt: (flat(bi, h, qi), 0, 0))
    const = lambda *shape: pl.BlockSpec(shape, lambda bi, h, qi, pt: (0,) * len(shape))
    hbm = pl.BlockSpec(memory_space=pl.ANY)
    page_buf = pltpu.VMEM((PAGE_SLOTS, pages_per_chunk, page_rows, VDIM), F32)
    score_buf = pltpu.VMEM((q_rows, pages_per_chunk * page_rows), F32)
    sspec_next = pl.BlockSpec(
        (1, rows, D_ATTN), lambda bi, h, qi, pt: (jnp.minimum(flat(bi, h, qi) + 1, nb - 1), 0, 0))
    grid_spec = pltpu.PrefetchScalarGridSpec(
        num_scalar_prefetch=1,
        grid=(b, N_HEADS, nq),
        in_specs=[qspec, qspec, kvspec, kvspec,
                  pl.BlockSpec((1, 2, TQ + NEAR_TILES * TK), lambda bi, h, qi, pt: (h, 0, 0)),
                  qspec,
                  sspec, sspec, sspec_next, sspec_next, nspec, nspec, sspec,
                  const(q_rows, LANES), const(q_rows, page_rows), const(q_rows, LANES),
                  const(1, HALF_DIM), const(1, HALF_DIM), const(1, HALF_DIM), const(1, HALF_DIM),
                  const(1, VDIM), hbm, hbm],
        out_specs=[qspec, sspec],
        scratch_shapes=[pltpu.VMEM((2 * TQ, TK), F32), pltpu.VMEM((2 * TQ, LANES), F32),
                        pltpu.VMEM((2 * TQ, 2 * LANES), F32),
                        pltpu.VMEM((TQ, NEAR_TILES * TK), F32),
                        page_buf, page_buf, pltpu.SemaphoreType.DMA((2, PAGE_SLOTS)),
                        score_buf, score_buf]
                       + [pltpu.VMEM((q_rows, LANES), F32)] * 3,
    )
    return pl.pallas_call(
        functools.partial(_attn_body, lam_init=lam_init),
        out_shape=[jax.ShapeDtypeStruct((b, t, D_ATTN), BF16),
                   jax.ShapeDtypeStruct((nb, rows, D_ATTN), F32)],
        grid_spec=grid_spec,
        compiler_params=pltpu.CompilerParams(
            dimension_semantics=("arbitrary", "arbitrary", "arbitrary"),
            vmem_limit_bytes=VMEM_LIMIT),
        name="attention",
    )(page_table.reshape(-1), qa, qb, kb, vb, bias, sz, sqa, sqb, sqa, sqb, kn, vn, ssz,
      hmask, lastb, newb,
      *lams, subln, cache_k, cache_v)


def _bias_by_distance(rel_bias, d):
    max_exact = N_BUCKETS // 2
    n = jnp.maximum(d, 0)
    n_f = jnp.maximum(n, 1).astype(F32)
    large = max_exact + (jnp.log(n_f / max_exact) / math.log(MAX_DISTANCE / max_exact)
                         * (N_BUCKETS - max_exact)).astype(jnp.int32)
    bucket = jnp.where(n < max_exact, n, jnp.minimum(large, N_BUCKETS - 1))
    table = rel_bias.astype(F32)
    vals = (table[bucket] - table[N_BUCKETS - 1]).T * LOG2_E
    return jnp.where(d[None, :] >= 0, vals, NEG_INF)


def _toeplitz(rel_bias, delta, n_q, n_k):
    w = n_q + n_k
    row = _bias_by_distance(rel_bias, delta + (n_q - 1) - jnp.arange(w))
    skew = jnp.tile(row, (1, n_q))[:, :n_q * (w - 1)].reshape(N_HEADS, n_q, w - 1)
    return skew[:, :, n_q - 1:n_q - 1 + n_k]


def kernel(x_prompt, x_sample, cache_k, cache_v, state_conv, page_table, c_prompt, c_sample,
           rel_bias, w_ada, b_ada, w_in, w_conv, lambda_q1, lambda_k1, lambda_q2, lambda_k2,
           subln_w, w_out, norm_f):
    depth = w_in.shape[0]
    assert depth == 1
    batch, seq, _ = x_prompt.shape
    nb, dec_seq, _ = x_sample.shape
    n_phys, page = cache_k.shape[1], cache_k.shape[2]
    lam_init = 0.8 - 0.6 * math.exp(-0.3 * 0)

    n_c = batch + nb
    n_c_pad = -(-n_c // MOD_ROWS_PAD) * MOD_ROWS_PAD
    c_all = jnp.concatenate([c_prompt, c_sample, jnp.zeros((n_c_pad - n_c, D_MODEL), F32)], axis=0)
    mod = _modulation(c_all, w_ada[0], b_ada[0].reshape(1, -1))
    mod_p = mod[:batch].reshape(batch, 1, 3 * D_MODEL)
    mod_s = mod[batch:n_c].reshape(nb, 1, 3 * D_MODEL)

    w_in_bf = w_in[0].astype(BF16)
    w_out_bf = w_out[0].astype(BF16)
    lams = [a.reshape(1, HALF_DIM) for a in (lambda_q1[0], lambda_k1[0], lambda_q2[0], lambda_k2[0])]
    subln = subln_w[0].reshape(1, VDIM)

    back = TQ - jnp.arange(TQ + NEAR_TILES * TK)
    bias_prompt = jnp.stack([_bias_by_distance(rel_bias, back),
                             _bias_by_distance(rel_bias, TK + back)], axis=1)

    q_rows = 2 * dec_seq * N_HEADS
    page_rows = page * N_HEADS
    row_head = jnp.arange(q_rows) // (2 * dec_seq)
    same_head = row_head[:, None] == (jnp.arange(page_rows) % N_HEADS)[None, :]
    head_mask = jnp.where(same_head, 0.0, NEG_INF).astype(F32)
    both_maps = lambda t: jnp.concatenate([t, t], axis=1).reshape(q_rows, t.shape[2])
    t_last = both_maps(_toeplitz(rel_bias, page, dec_seq, page))
    last_page = jnp.where(same_head, jnp.repeat(t_last, N_HEADS, axis=1), NEG_INF)
    t_new = both_maps(_toeplitz(rel_bias, 0, dec_seq, dec_seq))
    new_cols = dec_seq * N_HEADS
    bias_new = jnp.where(same_head[:, :new_cols], jnp.repeat(t_new, N_HEADS, axis=1), NEG_INF)
    bias_new = jnp.concatenate([bias_new, jnp.full((q_rows, LANES - new_cols), NEG_INF, F32)],
                               axis=1)

    qa, qb, kf, vf, kb, vb, sz, gp, st_p = _premix(
        x_prompt, mod_p, w_in_bf, w_conv[0], jnp.zeros((batch, CONV_WIDTH - 1, D_CONV), F32),
        g=1, r=ROW_TILE, attn_dtype=BF16, emit_bf16_kv=True)
    g_s = ROW_TILE // dec_seq
    qa_s, qb_s, kf_s, vf_s, sz_s, gs, st_s = _premix(
        x_sample, mod_s, w_in_bf, w_conv[0], state_conv[0],
        g=g_s, r=dec_seq, attn_dtype=F32, emit_bf16_kv=False)
    pages_k = cache_k.reshape(n_phys, page_rows, VDIM)
    pages_v = cache_v.reshape(n_phys, page_rows, VDIM)
    a_p, a_s = _attention(page_table, qa, qb, kb, vb, bias_prompt, sz, qa_s, qb_s, kf_s, vf_s, sz_s,
                          head_mask[:, :LANES], last_page, bias_new, lams, subln, pages_k, pages_v,
                          lam_init=lam_init)
    y_prompt = _postmix(a_p, gp, x_prompt, mod_p, w_out_bf, norm_f, g=1, r=ROW_TILE)
    y_sample = _postmix(a_s, gs, x_sample, mod_s, w_out_bf, norm_f, g=g_s, r=dec_seq)

    kv_p = lambda a: a.reshape(1, batch, seq, N_HEADS, VDIM)
    kv_s = lambda a: a.reshape(1, nb, dec_seq, N_HEADS, VDIM)
    return (y_prompt, y_sample, kv_p(kf), kv_p(vf), st_p[None], kv_s(kf_s), kv_s(vf_s), st_s[None])
```

```python
import functools
import math

import jax
import jax.numpy as jnp
from jax import lax
from jax.experimental import pallas as pl
from jax.experimental.pallas import tpu as pltpu

F32 = jnp.float32
BF16 = jnp.bfloat16

D_MODEL = 1024
D_ATTN = 512
D_CONV = 512
HALF_DIM = 64
VDIM = 128
N_HEADS = 4
CONV_WIDTH = 3
N_BUCKETS = 32
MAX_DISTANCE = 128
NORM_EPS = 1e-6
SUBLN_EPS = 1e-5
ATTN_SCALE = HALF_DIM ** -0.5
LOG2_E = math.log2(math.e)
NEG_INF = -1e30
D_IN = 4 * D_ATTN + 4 * D_CONV

LANES = 128
VMEM_LIMIT = 56 * 1024 * 1024

MOD_ROWS_PAD = 8
MOD_TN = 512
ROW_TILE = 512
POST_ROW_TILE = 1024
TQ = 512
TK = 256
NEAR_TILES = TQ // TK + 1
FAR_UNROLL = 4
CHUNKS_PER_SAMPLE = 8
PAGE_SLOTS = 8


def _silu(x):
    return x * jax.nn.sigmoid(x)


def _dot_nt(a, b):
    return lax.dot_general(a, b, (((1,), (1,)), ((), ())), preferred_element_type=F32)


def _lane_tile(x, n):
    return x if n == 1 else jnp.concatenate([x] * n, axis=1)


def _row_tile(x):
    return jnp.concatenate([x, x], axis=0)


def _mod_body(c_ref, w_ref, b_ref, o_ref):
    c = c_ref[...]
    o_ref[...] = jnp.dot(_silu(c).astype(BF16), w_ref[...].astype(BF16),
                         preferred_element_type=F32) + b_ref[...]


def _modulation(c_all, w_ada, b_ada):
    rows = c_all.shape[0]
    n = w_ada.shape[1]
    return pl.pallas_call(
        _mod_body,
        out_shape=jax.ShapeDtypeStruct((rows, n), F32),
        grid=(n // MOD_TN,),
        in_specs=[pl.BlockSpec((rows, D_MODEL), lambda j: (0, 0)),
                  pl.BlockSpec((D_MODEL, MOD_TN), lambda j: (0, j)),
                  pl.BlockSpec((1, MOD_TN), lambda j: (0, j))],
        out_specs=pl.BlockSpec((rows, MOD_TN), lambda j: (0, j)),
        compiler_params=pltpu.CompilerParams(dimension_semantics=("arbitrary",),
                                             vmem_limit_bytes=VMEM_LIMIT),
        name="adaln_mod",
    )(c_all, w_ada, b_ada)


def _premix_body(x_ref, shift_ref, scale_ref, w_ref, wconv_ref, prev_ref, *rest, g, r, attn_dtype,
                 emit_bf16_kv):
    if emit_bf16_kv:
        qa_ref, qb_ref, kf_ref, vf_ref, kb_ref, vb_ref, sz_ref, g_ref, st_ref, carry = rest
    else:
        qa_ref, qb_ref, kf_ref, vf_ref, sz_ref, g_ref, st_ref, carry = rest
    rows = g * r

    @pl.when(pl.program_id(1) == 0)
    def _():
        carry[...] = prev_ref[...]

    x = x_ref[...]
    xn = x * lax.rsqrt(jnp.mean(x * x, axis=-1, keepdims=True) + NORM_EPS)
    h = xn * (1.0 + scale_ref[...]) + shift_ref[...]
    h2 = h.reshape(rows, D_MODEL).astype(BF16)

    def proj(i):
        return jnp.dot(h2, w_ref[:, i * D_ATTN:(i + 1) * D_ATTN], preferred_element_type=F32)

    def put(ref, val):
        ref[...] = val.reshape(g, r, val.shape[-1]).astype(ref.dtype)

    q = proj(0) * (ATTN_SCALE * LOG2_E)
    lane = lax.broadcasted_iota(jnp.int32, (rows, D_ATTN), 1)
    first_half = (lane & (VDIM - 1)) < HALF_DIM
    put(qa_ref, jnp.where(first_half, q, 0.0))
    put(qb_ref, jnp.where(first_half, 0.0, q))
    def put_cache_rows(ref, val):
        val3 = val.reshape(g, r, D_ATTN)
        for hd in range(N_HEADS):
            ref[:, pl.ds(hd, r, stride=N_HEADS), :] = val3[:, :, hd * VDIM:(hd + 1) * VDIM]

    k = proj(1)
    put_cache_rows(kf_ref, k)
    v = proj(2)
    put_cache_rows(vf_ref, v)
    if emit_bf16_kv:
        put(kb_ref, k)
        put(vb_ref, v)
    put(sz_ref, _silu(proj(3)))

    b_gate = proj(4)
    u = proj(5) * proj(6)
    prev = carry[...]
    prev0 = jnp.broadcast_to(prev[:, 0:1, :], (g, r, D_CONV)).reshape(rows, D_CONV)
    prev1 = jnp.broadcast_to(prev[:, 1:2, :], (g, r, D_CONV)).reshape(rows, D_CONV)
    t = lax.broadcasted_iota(jnp.int32, (rows, D_CONV), 0) & (r - 1)
    u1 = jnp.where(t >= 1, pltpu.roll(u, 1, axis=0), prev1)
    u2 = jnp.where(t >= 2, pltpu.roll(u, 2, axis=0), jnp.where(t == 1, prev1, prev0))
    wc = wconv_ref[...]
    conv = wc[0:1, :] * u2 + wc[1:2, :] * u1 + wc[2:3, :] * u
    put(g_ref, b_gate * conv * _silu(proj(7)))

    last2 = u.reshape(g, r, D_CONV)[:, r - 2:, :]
    carry[...] = last2
    st_ref[...] = last2


def _premix(x3, mod3, w_in_bf, w_conv, prev_state, *, g, r, attn_dtype, emit_bf16_kv):
    n_groups, rows_per_group, _ = x3.shape
    grid = (n_groups // g, rows_per_group // r)
    assert r & (r - 1) == 0 and r >= 8
    act = lambda n: pl.BlockSpec((g, r, n), lambda o, i: (o, i, 0))
    mod_spec = lambda col: pl.BlockSpec((g, 1, D_MODEL), lambda o, i: (o, 0, col))
    sds = lambda dt: jax.ShapeDtypeStruct((n_groups, rows_per_group, D_ATTN), dt)
    cache_rows = jax.ShapeDtypeStruct((n_groups, rows_per_group * N_HEADS, VDIM), F32)
    out_shape = [sds(attn_dtype), sds(attn_dtype), cache_rows, cache_rows]
    if emit_bf16_kv:
        out_shape += [sds(BF16), sds(BF16)]
    out_shape += [sds(attn_dtype), sds(attn_dtype),
                  jax.ShapeDtypeStruct((n_groups, CONV_WIDTH - 1, D_CONV), F32)]
    out_specs = [act(D_ATTN)] * (len(out_shape) - 1)
    out_specs[2] = out_specs[3] = pl.BlockSpec((g, r * N_HEADS, VDIM), lambda o, i: (o, i, 0))
    out_specs.append(pl.BlockSpec((g, CONV_WIDTH - 1, D_CONV), lambda o, i: (o, 0, 0)))
    body = functools.partial(_premix_body, g=g, r=r, attn_dtype=attn_dtype,
                             emit_bf16_kv=emit_bf16_kv)
    return pl.pallas_call(
        body,
        out_shape=out_shape,
        grid=grid,
        in_specs=[act(D_MODEL), mod_spec(0), mod_spec(1),
                  pl.BlockSpec((D_MODEL, D_IN), lambda o, i: (0, 0)),
                  pl.BlockSpec((CONV_WIDTH, D_CONV), lambda o, i: (0, 0)),
                  pl.BlockSpec((g, CONV_WIDTH - 1, D_CONV), lambda o, i: (o, 0, 0))],
        out_specs=out_specs,
        scratch_shapes=[pltpu.VMEM((g, CONV_WIDTH - 1, D_CONV), F32)],
        compiler_params=pltpu.CompilerParams(dimension_semantics=("arbitrary", "arbitrary"),
                                             vmem_limit_bytes=VMEM_LIMIT),
        name="premix",
    )(x3, mod3, mod3, w_in_bf, w_conv, prev_state)


def _postmix_body(a_ref, g_ref, x_ref, gate_ref, w_ref, nf_ref, y_ref, *, g, r):
    rows = g * r
    a = a_ref[...].reshape(rows, D_ATTN).astype(BF16)
    gg = g_ref[...].reshape(rows, D_CONV).astype(BF16)
    mix = jnp.concatenate([a, gg], axis=1)
    out = jnp.dot(mix, w_ref[...], preferred_element_type=F32).reshape(g, r, D_MODEL)
    xo = x_ref[...] + gate_ref[...] * out
    y = xo * lax.rsqrt(jnp.mean(xo * xo, axis=-1, keepdims=True) + NORM_EPS)
    y_ref[...] = y * nf_ref[...]


def _postmix(a3, g3, x3, mod3, w_out_bf, norm_f, *, g, r):
    n_groups, rows_per_group, _ = x3.shape
    grid = (n_groups // g, rows_per_group // r)
    act = lambda n: pl.BlockSpec((g, r, n), lambda o, i: (o, i, 0))
    return pl.pallas_call(
        functools.partial(_postmix_body, g=g, r=r),
        out_shape=jax.ShapeDtypeStruct(x3.shape, F32),
        grid=grid,
        in_specs=[act(D_ATTN), act(D_CONV), act(D_MODEL),
                  pl.BlockSpec((g, 1, D_MODEL), lambda o, i: (o, 0, 2)),
                  pl.BlockSpec((D_ATTN + D_CONV, D_MODEL), lambda o, i: (0, 0)),
                  pl.BlockSpec((1, 1, D_MODEL), lambda o, i: (0, 0, 0))],
        out_specs=act(D_MODEL),
        compiler_params=pltpu.CompilerParams(dimension_semantics=("arbitrary", "arbitrary"),
                                             vmem_limit_bytes=VMEM_LIMIT),
        name="postmix",
    )(a3, g3, x3, mod3, w_out_bf, norm_f.reshape(1, 1, D_MODEL))


def _lambda(lq1_ref, lk1_ref, lq2_ref, lk2_ref, lam_init):
    l1 = jnp.sum(lq1_ref[...] * lk1_ref[...], axis=1, keepdims=True)
    l2 = jnp.sum(lq2_ref[...] * lk2_ref[...], axis=1, keepdims=True)
    return jnp.exp(l1) - jnp.exp(l2) + lam_init


def _tile_order(i, n):
    return jnp.where(i % 2 == 0, i // 2, n - 1 - i // 2)


def _sub_ln_gate(o, subln_ref, sz, lam_init):
    on = o * lax.rsqrt(jnp.mean(o * o, axis=-1, keepdims=True) + SUBLN_EPS)
    return on * (subln_ref[...] * (1.0 - lam_init)) * sz


def _attn_body(pt_ref,
               qa_ref, qb_ref, k_ref, v_ref, bias_ref, sz_ref,
               sqa_ref, sqb_ref, sqa_next_ref, sqb_next_ref, kn_ref, vn_ref, ssz_ref,
               hmask_ref, lastb_ref, newb_ref,
               lq1_ref, lk1_ref, lq2_ref, lk2_ref, subln_ref, k_hbm, v_hbm,
               o_ref, so_ref,
               s_scr, m_scr, acc_scr, bias_scr, k_buf, v_buf, sems, ss_even, ss_odd,
               sm_scr, sl_scr, sacc_scr,
               *, lam_init):
    order = pl.program_id(2)
    qi = _tile_order(order, pl.num_programs(2))
    step = (pl.program_id(0) * pl.num_programs(1) + pl.program_id(1)) * pl.num_programs(2) + order
    n_steps = pl.num_programs(0) * pl.num_programs(1) * pl.num_programs(2)
    lam = _lambda(lq1_ref, lk1_ref, lq2_ref, lk2_ref, lam_init)

    n_slots, n_pages = k_buf.shape[0], k_buf.shape[1]
    chunks = CHUNKS_PER_SAMPLE
    n_chunks = n_steps * chunks
    rows = sqa_ref.shape[1]

    def page_copies(hbm, buf, sem_row, chunk):
        slot = lax.rem(chunk, n_slots)
        return [pltpu.make_async_copy(hbm.at[pt_ref[chunk * n_pages + i]], buf.at[slot, i],
                                      sems.at[sem_row, slot]) for i in range(n_pages)]

    def k_copies(chunk):
        return page_copies(k_hbm, k_buf, 0, chunk)

    def v_copies(chunk):
        return page_copies(v_hbm, v_buf, 1, chunk)

    def start_chunk(chunk):
        for cp in k_copies(chunk) + v_copies(chunk):
            cp.start()

    def head_slice(h):
        return slice(h * VDIM, (h + 1) * VDIM)

    def stack_queries(a_ref, b_ref):
        return jnp.concatenate([r[0, :, head_slice(h)] for h in range(N_HEADS)
                                for r in (a_ref, b_ref)], axis=0).astype(BF16)

    def page_scores(q_all, chunk):
        slot = lax.rem(chunk, n_slots)
        return jnp.concatenate([_dot_nt(q_all, k_buf[slot, i].astype(BF16))
                                for i in range(n_pages)], axis=1)

    sq_all = stack_queries(sqa_ref, sqb_ref)
    sq_next = stack_queries(sqa_next_ref, sqb_next_ref)

    @pl.when(step == 0)
    def _():
        for ahead in range(n_slots - 1):
            start_chunk(ahead)
        for cp in k_copies(0):
            cp.wait()
        ss_even[...] = page_scores(sq_all, 0)

    sm_scr[...] = jnp.full(sm_scr.shape, NEG_INF, F32)
    sl_scr[...] = jnp.zeros(sl_scr.shape, F32)
    sacc_scr[...] = jnp.zeros(sacc_scr.shape, F32)

    def sample_update(s, weighted_values):
        m_prev = sm_scr[...]
        m_new = jnp.maximum(m_prev, jnp.max(s, axis=1, keepdims=True))
        alpha = jnp.exp2(m_prev - m_new)
        p = jnp.exp2(s - _lane_tile(m_new, s.shape[1] // LANES))
        sl_scr[...] = alpha * sl_scr[...] + jnp.sum(p, axis=1, keepdims=True)
        sacc_scr[...] = alpha * sacc_scr[...] + weighted_values(p.astype(BF16))
        sm_scr[...] = m_new

    def chunk_transfers(j):
        chunk = step * chunks + j
        ahead = chunk + (n_slots - 1)
        nxt = chunk + 1

        @pl.when(ahead < n_chunks)
        def _():
            start_chunk(ahead)

        @pl.when(nxt < n_chunks)
        def _():
            for cp in k_copies(nxt):
                cp.wait()

        for cp in v_copies(chunk):
            cp.wait()

    def sample_chunk(j, s_ref, s_next_ref):
        chunk = step * chunks + j
        slot = lax.rem(chunk, n_slots)
        nxt = chunk + 1
        q_next = jnp.where(j == chunks - 1, sq_next, sq_all)
        s_next = page_scores(q_next, jnp.minimum(nxt, n_chunks - 1))
        width = lastb_ref.shape[1]
        hmask = _lane_tile(hmask_ref[...], width // LANES)
        last = jnp.where(j == chunks - 1, lastb_ref[...], hmask)
        s = s_ref[...] + jnp.concatenate([hmask] * (n_pages - 1) + [last], axis=1)

        def weighted_values(pb):
            v_all = jnp.concatenate([v_buf[slot, i].astype(BF16) for i in range(n_pages)], axis=0)
            return jnp.dot(pb, v_all, preferred_element_type=F32)

        sample_update(s, weighted_values)
        s_next_ref[...] = s_next

    q2 = jnp.concatenate([qa_ref[0], qb_ref[0]], axis=0)
    m_scr[...] = jnp.full(m_scr.shape, NEG_INF, F32)
    acc_scr[...] = jnp.zeros(acc_scr.shape, F32)

    def scores(j):
        return _dot_nt(q2, k_ref[0, pl.ds(pl.multiple_of(j * TK, TK), TK), :])

    def absorb(s, j):
        v_t = v_ref[0, pl.ds(pl.multiple_of(j * TK, TK), TK), :]
        m_prev = m_scr[...]
        m_new = jnp.maximum(m_prev, jnp.max(s, axis=1, keepdims=True))
        alpha = jnp.exp2(m_prev - m_new)
        p = jnp.exp2((s - _lane_tile(m_new, TK // LANES)).astype(BF16))
        v_aug = jnp.concatenate([v_t, jnp.ones((TK, LANES), BF16)], axis=1)
        pv = jnp.dot(p, v_aug, preferred_element_type=F32)
        acc_scr[...] = acc_scr[...] * _lane_tile(alpha, 2) + pv
        m_scr[...] = m_new

    n_far = jnp.maximum(qi * (TQ // TK) - 1, 0)
    s_scr[...] = scores(0)

    @pl.when(order <= 1)
    def _():
        row = jnp.where(order == 0, bias_ref[0, 0:1, :], bias_ref[0, 1:2, :])
        skewed = pltpu.roll(jnp.broadcast_to(row, (TQ, row.shape[1])), 0, 1,
                            stride=1, stride_axis=0)
        bias_scr[...] = skewed[:, TQ:TQ + NEAR_TILES * TK]

    def far_tiles(j0, count):
        s = s_scr[...]
        for u in range(count):
            s_next = scores(j0 + u + 1)
            absorb(s, j0 + u)
            s = s_next
        s_scr[...] = s

    def far_single(j, carry):
        far_tiles(j, 1)
        return carry

    n_groups = n_far // FAR_UNROLL

    def interleaved(jj, carry):
        for parity, (s_ref, s_next_ref) in enumerate(((ss_even, ss_odd), (ss_odd, ss_even))):
            j = 2 * jj + parity
            chunk_transfers(j)
            group = (n_groups * j) // chunks
            has_group = (n_groups * (j + 1)) // chunks > group

            @pl.when(has_group)
            def _():
                far_tiles(group * FAR_UNROLL, FAR_UNROLL)
                sample_chunk(j, s_ref, s_next_ref)

            @pl.when(jnp.logical_not(has_group))
            def _():
                sample_chunk(j, s_ref, s_next_ref)
        return carry

    lax.fori_loop(0, chunks // 2, interleaved, 0)
    lax.fori_loop(n_groups * FAR_UNROLL, n_far, far_single, 0)
    s = s_scr[...]
    for u in range(NEAR_TILES):
        s_next = scores(n_far + u + 1) if u + 1 < NEAR_TILES else None
        absorb(s + _row_tile(bias_scr[:, u * TK:(u + 1) * TK]), n_far + u)
        s = s_next

    acc = acc_scr[...]
    o_all = acc[:, :VDIM] / acc[:, VDIM:]
    o = o_all[:TQ] - lam * o_all[TQ:]
    o_ref[0] = _sub_ln_gate(o, subln_ref, sz_ref[0].astype(F32), lam_init).astype(o_ref.dtype)

    def new_page(ref):
        pad = jnp.zeros((LANES - N_HEADS * rows, VDIM), F32)
        return jnp.concatenate([ref[0], pad], axis=0).astype(BF16)

    sample_update(_dot_nt(sq_all, new_page(kn_ref)) + newb_ref[...],
                  lambda pb: jnp.dot(pb, new_page(vn_ref), preferred_element_type=F32))
    so_all = sacc_scr[...] / sl_scr[...]
    for h in range(N_HEADS):
        r0 = 2 * rows * h
        so = so_all[r0:r0 + rows] - lam * so_all[r0 + rows:r0 + 2 * rows]
        so_ref[0, :, head_slice(h)] = _sub_ln_gate(so, subln_ref, ssz_ref[0, :, head_slice(h)],
                                                   lam_init)


def _attention(page_table, qa, qb, kb, vb, bias, sz, sqa, sqb, kn, vn, ssz, hmask, lastb, newb,
               lams, subln, cache_k, cache_v, *, lam_init):
    b, t, _ = qa.shape
    nb, rows, _ = sqa.shape
    n_phys, page_rows, _ = cache_k.shape
    nq = t // TQ
    q_rows = 2 * rows * N_HEADS
    pages_per_chunk = page_table.shape[1] // CHUNKS_PER_SAMPLE
    assert TQ % TK == 0 and t % TQ == 0
    assert b * N_HEADS * nq == nb, "one sample per prompt query tile"
    assert page_table.shape[1] % CHUNKS_PER_SAMPLE == 0 and CHUNKS_PER_SAMPLE % 2 == 0
    assert (nq * (TQ // TK) - 1) // FAR_UNROLL <= CHUNKS_PER_SAMPLE

    def flat(bi, h, qi):
        return (bi * N_HEADS + h) * nq + qi

    qspec = pl.BlockSpec((1, TQ, VDIM), lambda bi, h, qi, pt: (bi, _tile_order(qi, nq), h))
    kvspec = pl.BlockSpec((1, t, VDIM), lambda bi, h, qi, pt: (bi, 0, h))
    sspec = pl.BlockSpec((1, rows, D_ATTN), lambda bi, h, qi, pt: (flat(bi, h, qi), 0, 0))
    nspec = pl.BlockSpec((1, rows * N_HEADS, VDIM), lambda bi, h, qi, pt: (flat(bi, h, qi), 0, 0))
    const = lambda *shape: pl.BlockSpec(shape, lambda bi, h, qi, pt: (0,) * len(shape))
    hbm = pl.BlockSpec(memory_space=pl.ANY)
    page_buf = pltpu.VMEM((PAGE_SLOTS, pages_per_chunk, page_rows, VDIM), F32)
    score_buf = pltpu.VMEM((q_rows, pages_per_chunk * page_rows), F32)
    sspec_next = pl.BlockSpec(
        (1, rows, D_ATTN), lambda bi, h, qi, pt: (jnp.minimum(flat(bi, h, qi) + 1, nb - 1), 0, 0))
    grid_spec = pltpu.PrefetchScalarGridSpec(
        num_scalar_prefetch=1,
        grid=(b, N_HEADS, nq),
        in_specs=[qspec, qspec, kvspec, kvspec,
                  pl.BlockSpec((1, 2, TQ + NEAR_TILES * TK), lambda bi, h, qi, pt: (h, 0, 0)),
                  qspec,
                  sspec, sspec, sspec_next, sspec_next, nspec, nspec, sspec,
                  const(q_rows, LANES), const(q_rows, page_rows), const(q_rows, LANES),
                  const(1, HALF_DIM), const(1, HALF_DIM), const(1, HALF_DIM), const(1, HALF_DIM),
                  const(1, VDIM), hbm, hbm],
        out_specs=[qspec, sspec],
        scratch_shapes=[pltpu.VMEM((2 * TQ, TK), F32), pltpu.VMEM((2 * TQ, LANES), F32),
                        pltpu.VMEM((2 * TQ, 2 * LANES), F32),
                        pltpu.VMEM((TQ, NEAR_TILES * TK), F32),
                        page_buf, page_buf, pltpu.SemaphoreType.DMA((2, PAGE_SLOTS)),
                        score_buf, score_buf]
                       + [pltpu.VMEM((q_rows, LANES), F32)] * 3,
    )
    return pl.pallas_call(
        functools.partial(_attn_body, lam_init=lam_init),
        out_shape=[jax.ShapeDtypeStruct((b, t, D_ATTN), BF16),
                   jax.ShapeDtypeStruct((nb, rows, D_ATTN), F32)],
        grid_spec=grid_spec,
        compiler_params=pltpu.CompilerParams(
            dimension_semantics=("arbitrary", "arbitrary", "arbitrary"),
            vmem_limit_bytes=VMEM_LIMIT),
        name="attention",
    )(page_table.reshape(-1), qa, qb, kb, vb, bias, sz, sqa, sqb, sqa, sqb, kn, vn, ssz,
      hmask, lastb, newb,
      *lams, subln, cache_k, cache_v)


def _bias_by_distance(rel_bias, d):
    max_exact = N_BUCKETS // 2
    n = jnp.maximum(d, 0)
    n_f = jnp.maximum(n, 1).astype(F32)
    large = max_exact + (jnp.log(n_f / max_exact) / math.log(MAX_DISTANCE / max_exact)
                         * (N_BUCKETS - max_exact)).astype(jnp.int32)
    bucket = jnp.where(n < max_exact, n, jnp.minimum(large, N_BUCKETS - 1))
    table = rel_bias.astype(F32)
    vals = (table[bucket] - table[N_BUCKETS - 1]).T * LOG2_E
    return jnp.where(d[None, :] >= 0, vals, NEG_INF)


def _toeplitz(rel_bias, delta, n_q, n_k):
    w = n_q + n_k
    row = _bias_by_distance(rel_bias, delta + (n_q - 1) - jnp.arange(w))
    skew = jnp.tile(row, (1, n_q))[:, :n_q * (w - 1)].reshape(N_HEADS, n_q, w - 1)
    return skew[:, :, n_q - 1:n_q - 1 + n_k]


def kernel(x_prompt, x_sample, cache_k, cache_v, state_conv, page_table, c_prompt, c_sample,
           rel_bias, w_ada, b_ada, w_in, w_conv, lambda_q1, lambda_k1, lambda_q2, lambda_k2,
           subln_w, w_out, norm_f):
    depth = w_in.shape[0]
    assert depth == 1
    batch, seq, _ = x_prompt.shape
    nb, dec_seq, _ = x_sample.shape
    n_phys, page = cache_k.shape[1], cache_k.shape[2]
    lam_init = 0.8 - 0.6 * math.exp(-0.3 * 0)

    n_c = batch + nb
    n_c_pad = -(-n_c // MOD_ROWS_PAD) * MOD_ROWS_PAD
    c_all = jnp.concatenate([c_prompt, c_sample, jnp.zeros((n_c_pad - n_c, D_MODEL), F32)], axis=0)
    mod = _modulation(c_all, w_ada[0], b_ada[0].reshape(1, -1))
    mod_p = mod[:batch].reshape(batch, 1, 3 * D_MODEL)
    mod_s = mod[batch:n_c].reshape(nb, 1, 3 * D_MODEL)

    w_in_bf = w_in[0].astype(BF16)
    w_out_bf = w_out[0].astype(BF16)
    lams = [a.reshape(1, HALF_DIM) for a in (lambda_q1[0], lambda_k1[0], lambda_q2[0], lambda_k2[0])]
    subln = subln_w[0].reshape(1, VDIM)

    back = TQ - jnp.arange(TQ + NEAR_TILES * TK)
    bias_prompt = jnp.stack([_bias_by_distance(rel_bias, back),
                             _bias_by_distance(rel_bias, TK + back)], axis=1)

    q_rows = 2 * dec_seq * N_HEADS
    page_rows = page * N_HEADS
    row_head = jnp.arange(q_rows) // (2 * dec_seq)
    same_head = row_head[:, None] == (jnp.arange(page_rows) % N_HEADS)[None, :]
    head_mask = jnp.where(same_head, 0.0, NEG_INF).astype(F32)
    both_maps = lambda t: jnp.concatenate([t, t], axis=1).reshape(q_rows, t.shape[2])
    t_last = both_maps(_toeplitz(rel_bias, page, dec_seq, page))
    last_page = jnp.where(same_head, jnp.repeat(t_last, N_HEADS, axis=1), NEG_INF)
    t_new = both_maps(_toeplitz(rel_bias, 0, dec_seq, dec_seq))
    new_cols = dec_seq * N_HEADS
    bias_new = jnp.where(same_head[:, :new_cols], jnp.repeat(t_new, N_HEADS, axis=1), NEG_INF)
    bias_new = jnp.concatenate([bias_new, jnp.full((q_rows, LANES - new_cols), NEG_INF, F32)],
                               axis=1)

    qa, qb, kf, vf, kb, vb, sz, gp, st_p = _premix(
        x_prompt, mod_p, w_in_bf, w_conv[0], jnp.zeros((batch, CONV_WIDTH - 1, D_CONV), F32),
        g=1, r=ROW_TILE, attn_dtype=BF16, emit_bf16_kv=True)
    g_s = ROW_TILE // dec_seq
    qa_s, qb_s, kf_s, vf_s, sz_s, gs, st_s = _premix(
        x_sample, mod_s, w_in_bf, w_conv[0], state_conv[0],
        g=g_s, r=dec_seq, attn_dtype=F32, emit_bf16_kv=False)
    pages_k = cache_k.reshape(n_phys, page_rows, VDIM)
    pages_v = cache_v.reshape(n_phys, page_rows, VDIM)
    a_p, a_s = _attention(page_table, qa, qb, kb, vb, bias_prompt, sz, qa_s, qb_s, kf_s, vf_s, sz_s,
                          head_mask[:, :LANES], last_page, bias_new, lams, subln, pages_k, pages_v,
                          lam_init=lam_init)
    y_prompt = _postmix(a_p, gp, x_prompt, mod_p, w_out_bf, norm_f, g=1, r=POST_ROW_TILE)
    y_sample = _postmix(a_s, gs, x_sample, mod_s, w_out_bf, norm_f, g=POST_ROW_TILE // dec_seq,
                        r=dec_seq)

    kv_p = lambda a: a.reshape(1, batch, seq, N_HEADS, VDIM)
    kv_s = lambda a: a.reshape(1, nb, dec_seq, N_HEADS, VDIM)
    return (y_prompt, y_sample, kv_p(kf), kv_p(vf), st_p[None], kv_s(kf_s), kv_s(vf_s), st_s[None])
```

```python
import functools
import math

import jax
import jax.numpy as jnp
from jax import lax
from jax.experimental import pallas as pl
from jax.experimental.pallas import tpu as pltpu

F32 = jnp.float32
BF16 = jnp.bfloat16

D_MODEL = 1024
D_ATTN = 512
D_CONV = 512
HALF_DIM = 64
VDIM = 128
N_HEADS = 4
CONV_WIDTH = 3
N_BUCKETS = 32
MAX_DISTANCE = 128
NORM_EPS = 1e-6
SUBLN_EPS = 1e-5
ATTN_SCALE = HALF_DIM ** -0.5
LOG2_E = math.log2(math.e)
NEG_INF = -1e30
D_IN = 4 * D_ATTN + 4 * D_CONV

LANES = 128
VMEM_LIMIT = 56 * 1024 * 1024

MOD_ROWS_PAD = 8
MOD_TN = 512
ROW_TILE = 512
POST_ROW_TILE = 1024
TQ = 512
TK = 256
NEAR_TILES = TQ // TK + 1
FAR_UNROLL = 4
CHUNKS_PER_SAMPLE = 8
PAGE_SLOTS = 8


def _silu(x):
    return x * jax.nn.sigmoid(x)


def _dot_nt(a, b):
    return lax.dot_general(a, b, (((1,), (1,)), ((), ())), preferred_element_type=F32)


def _lane_tile(x, n):
    return x if n == 1 else jnp.concatenate([x] * n, axis=1)


def _row_tile(x):
    return jnp.concatenate([x, x], axis=0)


def _mod_body(c_ref, w_ref, b_ref, o_ref):
    c = c_ref[...]
    o_ref[...] = jnp.dot(_silu(c).astype(BF16), w_ref[...].astype(BF16),
                         preferred_element_type=F32) + b_ref[...]


def _modulation(c_all, w_ada, b_ada):
    rows = c_all.shape[0]
    n = w_ada.shape[1]
    return pl.pallas_call(
        _mod_body,
        out_shape=jax.ShapeDtypeStruct((rows, n), F32),
        grid=(n // MOD_TN,),
        in_specs=[pl.BlockSpec((rows, D_MODEL), lambda j: (0, 0)),
                  pl.BlockSpec((D_MODEL, MOD_TN), lambda j: (0, j)),
                  pl.BlockSpec((1, MOD_TN), lambda j: (0, j))],
        out_specs=pl.BlockSpec((rows, MOD_TN), lambda j: (0, j)),
        compiler_params=pltpu.CompilerParams(dimension_semantics=("arbitrary",),
                                             vmem_limit_bytes=VMEM_LIMIT),
        name="adaln_mod",
    )(c_all, w_ada, b_ada)


def _premix_body(x_ref, shift_ref, scale_ref, w_ref, wconv_ref, prev_ref, *rest, g, r, attn_dtype,
                 emit_bf16_kv):
    if emit_bf16_kv:
        qa_ref, qb_ref, kf_ref, vf_ref, kb_ref, vb_ref, sz_ref, g_ref, st_ref, carry = rest
    else:
        qa_ref, qb_ref, kf_ref, vf_ref, sz_ref, g_ref, st_ref, carry = rest
    rows = g * r

    @pl.when(pl.program_id(1) == 0)
    def _():
        carry[...] = prev_ref[...]

    x = x_ref[...]
    xn = x * lax.rsqrt(jnp.mean(x * x, axis=-1, keepdims=True) + NORM_EPS)
    h = xn * (1.0 + scale_ref[...]) + shift_ref[...]
    h2 = h.reshape(rows, D_MODEL).astype(BF16)

    def proj(i):
        return jnp.dot(h2, w_ref[:, i * D_ATTN:(i + 1) * D_ATTN], preferred_element_type=F32)

    def put(ref, val):
        ref[...] = val.reshape(g, r, val.shape[-1]).astype(ref.dtype)

    q = proj(0) * (ATTN_SCALE * LOG2_E)
    lane = lax.broadcasted_iota(jnp.int32, (rows, D_ATTN), 1)
    first_half = (lane & (VDIM - 1)) < HALF_DIM
    put(qa_ref, jnp.where(first_half, q, 0.0))
    put(qb_ref, jnp.where(first_half, 0.0, q))
    def put_cache_rows(ref, val):
        val3 = val.reshape(g, r, D_ATTN)
        for hd in range(N_HEADS):
            ref[:, pl.ds(hd, r, stride=N_HEADS), :] = val3[:, :, hd * VDIM:(hd + 1) * VDIM]

    k = proj(1)
    put_cache_rows(kf_ref, k)
    v = proj(2)
    put_cache_rows(vf_ref, v)
    if emit_bf16_kv:
        put(kb_ref, k)
        put(vb_ref, v)
    put(sz_ref, _silu(proj(3)))

    b_gate = proj(4)
    u = proj(5) * proj(6)
    prev = carry[...]
    prev0 = jnp.broadcast_to(prev[:, 0:1, :], (g, r, D_CONV)).reshape(rows, D_CONV)
    prev1 = jnp.broadcast_to(prev[:, 1:2, :], (g, r, D_CONV)).reshape(rows, D_CONV)
    t = lax.broadcasted_iota(jnp.int32, (rows, D_CONV), 0) & (r - 1)
    u1 = jnp.where(t >= 1, pltpu.roll(u, 1, axis=0), prev1)
    u2 = jnp.where(t >= 2, pltpu.roll(u, 2, axis=0), jnp.where(t == 1, prev1, prev0))
    wc = wconv_ref[...]
    conv = wc[0:1, :] * u2 + wc[1:2, :] * u1 + wc[2:3, :] * u
    put(g_ref, b_gate * conv * _silu(proj(7)))

    last2 = u.reshape(g, r, D_CONV)[:, r - 2:, :]
    carry[...] = last2
    st_ref[...] = last2


def _premix(x3, mod3, w_in_bf, w_conv, prev_state, *, g, r, attn_dtype, emit_bf16_kv):
    n_groups, rows_per_group, _ = x3.shape
    grid = (n_groups // g, rows_per_group // r)
    assert r & (r - 1) == 0 and r >= 8
    act = lambda n: pl.BlockSpec((g, r, n), lambda o, i: (o, i, 0))
    mod_spec = lambda col: pl.BlockSpec((g, 1, D_MODEL), lambda o, i: (o, 0, col))
    sds = lambda dt: jax.ShapeDtypeStruct((n_groups, rows_per_group, D_ATTN), dt)
    cache_rows = jax.ShapeDtypeStruct((n_groups, rows_per_group * N_HEADS, VDIM), F32)
    out_shape = [sds(attn_dtype), sds(attn_dtype), cache_rows, cache_rows]
    if emit_bf16_kv:
        out_shape += [sds(BF16), sds(BF16)]
    out_shape += [sds(attn_dtype), sds(attn_dtype),
                  jax.ShapeDtypeStruct((n_groups, CONV_WIDTH - 1, D_CONV), F32)]
    out_specs = [act(D_ATTN)] * (len(out_shape) - 1)
    out_specs[2] = out_specs[3] = pl.BlockSpec((g, r * N_HEADS, VDIM), lambda o, i: (o, i, 0))
    out_specs.append(pl.BlockSpec((g, CONV_WIDTH - 1, D_CONV), lambda o, i: (o, 0, 0)))
    body = functools.partial(_premix_body, g=g, r=r, attn_dtype=attn_dtype,
                             emit_bf16_kv=emit_bf16_kv)
    return pl.pallas_call(
        body,
        out_shape=out_shape,
        grid=grid,
        in_specs=[act(D_MODEL), mod_spec(0), mod_spec(1),
                  pl.BlockSpec((D_MODEL, D_IN), lambda o, i: (0, 0)),
                  pl.BlockSpec((CONV_WIDTH, D_CONV), lambda o, i: (0, 0)),
                  pl.BlockSpec((g, CONV_WIDTH - 1, D_CONV), lambda o, i: (o, 0, 0))],
        out_specs=out_specs,
        scratch_shapes=[pltpu.VMEM((g, CONV_WIDTH - 1, D_CONV), F32)],
        compiler_params=pltpu.CompilerParams(dimension_semantics=("arbitrary", "arbitrary"),
                                             vmem_limit_bytes=VMEM_LIMIT),
        name="premix",
    )(x3, mod3, mod3, w_in_bf, w_conv, prev_state)


def _postmix_body(a_ref, g_ref, x_ref, gate_ref, w_ref, nf_ref, y_ref, *, g, r):
    rows = g * r
    a = a_ref[...].reshape(rows, D_ATTN).astype(BF16)
    gg = g_ref[...].reshape(rows, D_CONV).astype(BF16)
    mix = jnp.concatenate([a, gg], axis=1)
    out = jnp.dot(mix, w_ref[...], preferred_element_type=F32).reshape(g, r, D_MODEL)
    xo = x_ref[...] + gate_ref[...] * out
    y = xo * lax.rsqrt(jnp.mean(xo * xo, axis=-1, keepdims=True) + NORM_EPS)
    y_ref[...] = y * nf_ref[...]


def _postmix(a3, g3, x3, mod3, w_out_bf, norm_f, *, g, r):
    n_groups, rows_per_group, _ = x3.shape
    grid = (n_groups // g, rows_per_group // r)
    act = lambda n: pl.BlockSpec((g, r, n), lambda o, i: (o, i, 0))
    return pl.pallas_call(
        functools.partial(_postmix_body, g=g, r=r),
        out_shape=jax.ShapeDtypeStruct(x3.shape, F32),
        grid=grid,
        in_specs=[act(D_ATTN), act(D_CONV), act(D_MODEL),
                  pl.BlockSpec((g, 1, D_MODEL), lambda o, i: (o, 0, 2)),
                  pl.BlockSpec((D_ATTN + D_CONV, D_MODEL), lambda o, i: (0, 0)),
                  pl.BlockSpec((1, 1, D_MODEL), lambda o, i: (0, 0, 0))],
        out_specs=act(D_MODEL),
        compiler_params=pltpu.CompilerParams(dimension_semantics=("arbitrary", "arbitrary"),
                                             vmem_limit_bytes=VMEM_LIMIT),
        name="postmix",
    )(a3, g3, x3, mod3, w_out_bf, norm_f.reshape(1, 1, D_MODEL))


def _lambda(lq1_ref, lk1_ref, lq2_ref, lk2_ref, lam_init):
    l1 = jnp.sum(lq1_ref[...] * lk1_ref[...], axis=1, keepdims=True)
    l2 = jnp.sum(lq2_ref[...] * lk2_ref[...], axis=1, keepdims=True)
    return jnp.exp(l1) - jnp.exp(l2) + lam_init


def _tile_order(i, n):
    return jnp.where(i % 2 == 0, i // 2, n - 1 - i // 2)


def _sub_ln_gate(o, subln_ref, sz, lam_init):
    on = o * lax.rsqrt(jnp.mean(o * o, axis=-1, keepdims=True) + SUBLN_EPS)
    return on * (subln_ref[...] * (1.0 - lam_init)) * sz


def _attn_body(pt_ref,
               qa_ref, qb_ref, k_ref, v_ref, bias_ref, sz_ref,
               sqa_ref, sqb_ref, sqa_next_ref, sqb_next_ref, kn_ref, vn_ref, ssz_ref,
               hmask_ref, lastb_ref, newb_ref,
               lq1_ref, lk1_ref, lq2_ref, lk2_ref, subln_ref, k_hbm, v_hbm,
               o_ref, so_ref,
               s_scr, m_scr, acc_scr, bias_scr, k_buf, v_buf, sems, ss_even, ss_odd,
               sm_scr, sl_scr, sacc_scr,
               *, lam_init):
    order = pl.program_id(2)
    qi = _tile_order(order, pl.num_programs(2))
    step = (pl.program_id(0) * pl.num_programs(1) + pl.program_id(1)) * pl.num_programs(2) + order
    n_steps = pl.num_programs(0) * pl.num_programs(1) * pl.num_programs(2)
    lam = _lambda(lq1_ref, lk1_ref, lq2_ref, lk2_ref, lam_init)

    n_slots, n_pages = k_buf.shape[0], k_buf.shape[1]
    chunks = CHUNKS_PER_SAMPLE
    n_chunks = n_steps * chunks
    rows = sqa_ref.shape[1]

    def page_copies(hbm, buf, sem_row, chunk):
        slot = lax.rem(chunk, n_slots)
        return [pltpu.make_async_copy(hbm.at[pt_ref[chunk * n_pages + i]], buf.at[slot, i],
                                      sems.at[sem_row, slot]) for i in range(n_pages)]

    def k_copies(chunk):
        return page_copies(k_hbm, k_buf, 0, chunk)

    def v_copies(chunk):
        return page_copies(v_hbm, v_buf, 1, chunk)

    def start_chunk(chunk):
        for cp in k_copies(chunk) + v_copies(chunk):
            cp.start()

    def head_slice(h):
        return slice(h * VDIM, (h + 1) * VDIM)

    def stack_queries(a_ref, b_ref):
        return jnp.concatenate([r[0, :, head_slice(h)] for h in range(N_HEADS)
                                for r in (a_ref, b_ref)], axis=0).astype(BF16)

    def page_scores(q_all, chunk):
        slot = lax.rem(chunk, n_slots)
        return jnp.concatenate([_dot_nt(q_all, k_buf[slot, i].astype(BF16))
                                for i in range(n_pages)], axis=1)

    sq_all = stack_queries(sqa_ref, sqb_ref)
    sq_next = stack_queries(sqa_next_ref, sqb_next_ref)

    @pl.when(step == 0)
    def _():
        for ahead in range(n_slots - 1):
            start_chunk(ahead)
        for cp in k_copies(0):
            cp.wait()
        ss_even[...] = page_scores(sq_all, 0)

    sm_scr[...] = jnp.full(sm_scr.shape, NEG_INF, F32)
    sl_scr[...] = jnp.zeros(sl_scr.shape, F32)
    sacc_scr[...] = jnp.zeros(sacc_scr.shape, F32)

    def sample_update(s, weighted_values):
        m_prev = sm_scr[...]
        m_new = jnp.maximum(m_prev, jnp.max(s, axis=1, keepdims=True))
        alpha = jnp.exp2(m_prev - m_new)
        p = jnp.exp2(s - _lane_tile(m_new, s.shape[1] // LANES))
        sl_scr[...] = alpha * sl_scr[...] + jnp.sum(p, axis=1, keepdims=True)
        sacc_scr[...] = alpha * sacc_scr[...] + weighted_values(p.astype(BF16))
        sm_scr[...] = m_new

    def chunk_transfers(j):
        chunk = step * chunks + j
        ahead = chunk + (n_slots - 1)
        nxt = chunk + 1

        @pl.when(ahead < n_chunks)
        def _():
            start_chunk(ahead)

        @pl.when(nxt < n_chunks)
        def _():
            for cp in k_copies(nxt):
                cp.wait()

        for cp in v_copies(chunk):
            cp.wait()

    def sample_chunk(j, s_ref, s_next_ref):
        chunk = step * chunks + j
        slot = lax.rem(chunk, n_slots)
        nxt = chunk + 1
        q_next = jnp.where(j == chunks - 1, sq_next, sq_all)
        s_next = page_scores(q_next, jnp.minimum(nxt, n_chunks - 1))
        width = lastb_ref.shape[1]
        hmask = _lane_tile(hmask_ref[...], width // LANES)
        last = jnp.where(j == chunks - 1, lastb_ref[...], hmask)
        s = s_ref[...] + jnp.concatenate([hmask] * (n_pages - 1) + [last], axis=1)

        def weighted_values(pb):
            v_all = jnp.concatenate([v_buf[slot, i].astype(BF16) for i in range(n_pages)], axis=0)
            return jnp.dot(pb, v_all, preferred_element_type=F32)

        sample_update(s, weighted_values)
        s_next_ref[...] = s_next

    q2 = jnp.concatenate([qa_ref[0], qb_ref[0]], axis=0)
    m_scr[...] = jnp.full(m_scr.shape, NEG_INF, F32)
    acc_scr[...] = jnp.zeros(acc_scr.shape, F32)

    def scores(j):
        return _dot_nt(q2, k_ref[0, pl.ds(pl.multiple_of(j * TK, TK), TK), :])

    def absorb(s, j):
        v_t = v_ref[0, pl.ds(pl.multiple_of(j * TK, TK), TK), :]
        m_prev = m_scr[...]
        m_new = jnp.maximum(m_prev, jnp.max(s, axis=1, keepdims=True))
        alpha = jnp.exp2(m_prev - m_new)
        p = jnp.exp2((s - _lane_tile(m_new, TK // LANES)).astype(BF16))
        v_aug = jnp.concatenate([v_t, jnp.ones((TK, LANES), BF16)], axis=1)
        pv = jnp.dot(p, v_aug, preferred_element_type=F32)
        acc_scr[...] = acc_scr[...] * _lane_tile(alpha, 2) + pv
        m_scr[...] = m_new

    n_far = jnp.maximum(qi * (TQ // TK) - 1, 0)
    s_scr[...] = scores(0)

    @pl.when(order <= 1)
    def _():
        row = jnp.where(order == 0, bias_ref[0, 0:1, :], bias_ref[0, 1:2, :])
        skewed = pltpu.roll(jnp.broadcast_to(row, (TQ, row.shape[1])), 0, 1,
                            stride=1, stride_axis=0)
        bias_scr[...] = skewed[:, TQ:TQ + NEAR_TILES * TK]

    def far_tiles(j0, count):
        s = s_scr[...]
        for u in range(count):
            s_next = scores(j0 + u + 1)
            absorb(s, j0 + u)
            s = s_next
        s_scr[...] = s

    def far_single(j, carry):
        far_tiles(j, 1)
        return carry

    n_groups = n_far // FAR_UNROLL

    def interleaved(jj, carry):
        for parity, (s_ref, s_next_ref) in enumerate(((ss_even, ss_odd), (ss_odd, ss_even))):
            j = 2 * jj + parity
            chunk_transfers(j)
            group = (n_groups * j) // chunks
            has_group = (n_groups * (j + 1)) // chunks > group

            @pl.when(has_group)
            def _():
                sample_chunk(j, s_ref, s_next_ref)
                far_tiles(group * FAR_UNROLL, FAR_UNROLL)

            @pl.when(jnp.logical_not(has_group))
            def _():
                sample_chunk(j, s_ref, s_next_ref)
        return carry

    lax.fori_loop(0, chunks // 2, interleaved, 0)
    lax.fori_loop(n_groups * FAR_UNROLL, n_far, far_single, 0)
    s = s_scr[...]
    for u in range(NEAR_TILES):
        s_next = scores(n_far + u + 1) if u + 1 < NEAR_TILES else None
        absorb(s + _row_tile(bias_scr[:, u * TK:(u + 1) * TK]), n_far + u)
        s = s_next

    acc = acc_scr[...]
    o_all = acc[:, :VDIM] / acc[:, VDIM:]
    o = o_all[:TQ] - lam * o_all[TQ:]
    o_ref[0] = _sub_ln_gate(o, subln_ref, sz_ref[0].astype(F32), lam_init).astype(o_ref.dtype)

    def new_page(ref):
        pad = jnp.zeros((LANES - N_HEADS * rows, VDIM), F32)
        return jnp.concatenate([ref[0], pad], axis=0).astype(BF16)

    sample_update(_dot_nt(sq_all, new_page(kn_ref)) + newb_ref[...],
                  lambda pb: jnp.dot(pb, new_page(vn_ref), preferred_element_type=F32))
    so_all = sacc_scr[...] / sl_scr[...]
    for h in range(N_HEADS):
        r0 = 2 * rows * h
        so = so_all[r0:r0 + rows] - lam * so_all[r0 + rows:r0 + 2 * rows]
        so_ref[0, :, head_slice(h)] = _sub_ln_gate(so, subln_ref, ssz_ref[0, :, head_slice(h)],
                                                   lam_init)


def _attention(page_table, qa, qb, kb, vb, bias, sz, sqa, sqb, kn, vn, ssz, hmask, lastb, newb,
               lams, subln, cache_k, cache_v, *, lam_init):
    b, t, _ = qa.shape
    nb, rows, _ = sqa.shape
    n_phys, page_rows, _ = cache_k.shape
    nq = t // TQ
    q_rows = 2 * rows * N_HEADS
    pages_per_chunk = page_table.shape[1] // CHUNKS_PER_SAMPLE
    assert TQ % TK == 0 and t % TQ == 0
    assert b * N_HEADS * nq == nb, "one sample per prompt query tile"
    assert page_table.shape[1] % CHUNKS_PER_SAMPLE == 0 and CHUNKS_PER_SAMPLE % 2 == 0
    assert (nq * (TQ // TK) - 1) // FAR_UNROLL <= CHUNKS_PER_SAMPLE

    def flat(bi, h, qi):
        return (bi * N_HEADS + h) * nq + qi

    qspec = pl.BlockSpec((1, TQ, VDIM), lambda bi, h, qi, pt: (bi, _tile_order(qi, nq), h))
    kvspec = pl.BlockSpec((1, t, VDIM), lambda bi, h, qi, pt: (bi, 0, h))
    sspec = pl.BlockSpec((1, rows, D_ATTN), lambda bi, h, qi, pt: (flat(bi, h, qi), 0, 0))
    nspec = pl.BlockSpec((1, rows * N_HEADS, VDIM), lambda bi, h, qi, pt: (flat(bi, h, qi), 0, 0))
    const = lambda *shape: pl.BlockSpec(shape, lambda bi, h, qi, pt: (0,) * len(shape))
    hbm = pl.BlockSpec(memory_space=pl.ANY)
    page_buf = pltpu.VMEM((PAGE_SLOTS, pages_per_chunk, page_rows, VDIM), F32)
    score_buf = pltpu.VMEM((q_rows, pages_per_chunk * page_rows), F32)
    sspec_next = pl.BlockSpec(
        (1, rows, D_ATTN), lambda bi, h, qi, pt: (jnp.minimum(flat(bi, h, qi) + 1, nb - 1), 0, 0))
    grid_spec = pltpu.PrefetchScalarGridSpec(
        num_scalar_prefetch=1,
        grid=(b, N_HEADS, nq),
        in_specs=[qspec, qspec, kvspec, kvspec,
                  pl.BlockSpec((1, 2, TQ + NEAR_TILES * TK), lambda bi, h, qi, pt: (h, 0, 0)),
                  qspec,
                  sspec, sspec, sspec_next, sspec_next, nspec, nspec, sspec,
                  const(q_rows, LANES), const(q_rows, page_rows), const(q_rows, LANES),
                  const(1, HALF_DIM), const(1, HALF_DIM), const(1, HALF_DIM), const(1, HALF_DIM),
                  const(1, VDIM), hbm, hbm],
        out_specs=[qspec, sspec],
        scratch_shapes=[pltpu.VMEM((2 * TQ, TK), F32), pltpu.VMEM((2 * TQ, LANES), F32),
                        pltpu.VMEM((2 * TQ, 2 * LANES), F32),
                        pltpu.VMEM((TQ, NEAR_TILES * TK), F32),
                        page_buf, page_buf, pltpu.SemaphoreType.DMA((2, PAGE_SLOTS)),
                        score_buf, score_buf]
                       + [pltpu.VMEM((q_rows, LANES), F32)] * 3,
    )
    return pl.pallas_call(
        functools.partial(_attn_body, lam_init=lam_init),
        out_shape=[jax.ShapeDtypeStruct((b, t, D_ATTN), BF16),
                   jax.ShapeDtypeStruct((nb, rows, D_ATTN), F32)],
        grid_spec=grid_spec,
        compiler_params=pltpu.CompilerParams(
            dimension_semantics=("arbitrary", "arbitrary", "arbitrary"),
            vmem_limit_bytes=VMEM_LIMIT),
        name="attention",
    )(page_table.reshape(-1), qa, qb, kb, vb, bias, sz, sqa, sqb, sqa, sqb, kn, vn, ssz,
      hmask, lastb, newb,
      *lams, subln, cache_k, cache_v)


def _bias_by_distance(rel_bias, d):
    max_exact = N_BUCKETS // 2
    n = jnp.maximum(d, 0)
    n_f = jnp.maximum(n, 1).astype(F32)
    large = max_exact + (jnp.log(n_f / max_exact) / math.log(MAX_DISTANCE / max_exact)
                         * (N_BUCKETS - max_exact)).astype(jnp.int32)
    bucket = jnp.where(n < max_exact, n, jnp.minimum(large, N_BUCKETS - 1))
    table = rel_bias.astype(F32)
    vals = (table[bucket] - table[N_BUCKETS - 1]).T * LOG2_E
    return jnp.where(d[None, :] >= 0, vals, NEG_INF)


def _toeplitz(rel_bias, delta, n_q, n_k):
    w = n_q + n_k
    row = _bias_by_distance(rel_bias, delta + (n_q - 1) - jnp.arange(w))
    skew = jnp.tile(row, (1, n_q))[:, :n_q * (w - 1)].reshape(N_HEADS, n_q, w - 1)
    return skew[:, :, n_q - 1:n_q - 1 + n_k]


def kernel(x_prompt, x_sample, cache_k, cache_v, state_conv, page_table, c_prompt, c_sample,
           rel_bias, w_ada, b_ada, w_in, w_conv, lambda_q1, lambda_k1, lambda_q2, lambda_k2,
           subln_w, w_out, norm_f):
    depth = w_in.shape[0]
    assert depth == 1
    batch, seq, _ = x_prompt.shape
    nb, dec_seq, _ = x_sample.shape
    n_phys, page = cache_k.shape[1], cache_k.shape[2]
    lam_init = 0.8 - 0.6 * math.exp(-0.3 * 0)

    n_c = batch + nb
    n_c_pad = -(-n_c // MOD_ROWS_PAD) * MOD_ROWS_PAD
    c_all = jnp.concatenate([c_prompt, c_sample, jnp.zeros((n_c_pad - n_c, D_MODEL), F32)], axis=0)
    mod = _modulation(c_all, w_ada[0], b_ada[0].reshape(1, -1))
    mod_p = mod[:batch].reshape(batch, 1, 3 * D_MODEL)
    mod_s = mod[batch:n_c].reshape(nb, 1, 3 * D_MODEL)

    w_in_bf = w_in[0].astype(BF16)
    w_out_bf = w_out[0].astype(BF16)
    lams = [a.reshape(1, HALF_DIM) for a in (lambda_q1[0], lambda_k1[0], lambda_q2[0], lambda_k2[0])]
    subln = subln_w[0].reshape(1, VDIM)

    back = TQ - jnp.arange(TQ + NEAR_TILES * TK)
    bias_prompt = jnp.stack([_bias_by_distance(rel_bias, back),
                             _bias_by_distance(rel_bias, TK + back)], axis=1)

    q_rows = 2 * dec_seq * N_HEADS
    page_rows = page * N_HEADS
    row_head = jnp.arange(q_rows) // (2 * dec_seq)
    same_head = row_head[:, None] == (jnp.arange(page_rows) % N_HEADS)[None, :]
    head_mask = jnp.where(same_head, 0.0, NEG_INF).astype(F32)
    both_maps = lambda t: jnp.concatenate([t, t], axis=1).reshape(q_rows, t.shape[2])
    t_last = both_maps(_toeplitz(rel_bias, page, dec_seq, page))
    last_page = jnp.where(same_head, jnp.repeat(t_last, N_HEADS, axis=1), NEG_INF)
    t_new = both_maps(_toeplitz(rel_bias, 0, dec_seq, dec_seq))
    new_cols = dec_seq * N_HEADS
    bias_new = jnp.where(same_head[:, :new_cols], jnp.repeat(t_new, N_HEADS, axis=1), NEG_INF)
    bias_new = jnp.concatenate([bias_new, jnp.full((q_rows, LANES - new_cols), NEG_INF, F32)],
                               axis=1)

    qa, qb, kf, vf, kb, vb, sz, gp, st_p = _premix(
        x_prompt, mod_p, w_in_bf, w_conv[0], jnp.zeros((batch, CONV_WIDTH - 1, D_CONV), F32),
        g=1, r=ROW_TILE, attn_dtype=BF16, emit_bf16_kv=True)
    g_s = ROW_TILE // dec_seq
    qa_s, qb_s, kf_s, vf_s, sz_s, gs, st_s = _premix(
        x_sample, mod_s, w_in_bf, w_conv[0], state_conv[0],
        g=g_s, r=dec_seq, attn_dtype=F32, emit_bf16_kv=False)
    pages_k = cache_k.reshape(n_phys, page_rows, VDIM)
    pages_v = cache_v.reshape(n_phys, page_rows, VDIM)
    a_p, a_s = _attention(page_table, qa, qb, kb, vb, bias_prompt, sz, qa_s, qb_s, kf_s, vf_s, sz_s,
                          head_mask[:, :LANES], last_page, bias_new, lams, subln, pages_k, pages_v,
                          lam_init=lam_init)
    y_prompt = _postmix(a_p, gp, x_prompt, mod_p, w_out_bf, norm_f, g=1, r=POST_ROW_TILE)
    y_sample = _postmix(a_s, gs, x_sample, mod_s, w_out_bf, norm_f, g=g_s, r=dec_seq)

    kv_p = lambda a: a.reshape(1, batch, seq, N_HEADS, VDIM)
    kv_s = lambda a: a.reshape(1, nb, dec_seq, N_HEADS, VDIM)
    return (y_prompt, y_sample, kv_p(kf), kv_p(vf), st_p[None], kv_s(kf_s), kv_s(vf_s), st_s[None])
```
